```python
import jax, jax.numpy as jnp
from jax import lax
import numpy as np

D_MODEL = 1024
BATCH = 8
SEQ = 8192
DEPTH = 2

GRID_W = 64
NA_HEADS = 8
NA_HEAD_DIM = 64
NA_WIN_R = 8
NA_WIN_C = 16
NA_W = NA_HEADS * NA_HEAD_DIM
MLA_HEADS = 4
MLA_Q_RANK = 384
MLA_KV_RANK = 256
MLA_NOPE = 128
MLA_ROPE = 64
MLA_V = 128
MLA_QK = MLA_NOPE + MLA_ROPE
MLA_W = MLA_HEADS * MLA_V
GQA_HEADS = 8
GQA_KV_HEADS = 2
GQA_HEAD_DIM = 64
GQA_WINDOW = 128
GQA_Q_W = GQA_HEADS * GQA_HEAD_DIM
GQA_KV_W = GQA_KV_HEADS * GQA_HEAD_DIM
BLOCK = 128
N_BRANCH = 3
D_FF = 2816
CONV_W = 3
ROPE_THETA = 10000.0
EPS = 1e-6
NEG_INF = -1e30
IN_SPLITS = (3 * NA_W, MLA_Q_RANK, MLA_KV_RANK, MLA_ROPE, GQA_Q_W, 2 * GQA_KV_W, N_BRANCH * D_MODEL)
IN_W = 3 * NA_W + MLA_Q_RANK + MLA_KV_RANK + MLA_ROPE + GQA_Q_W + 2 * GQA_KV_W + N_BRANCH * D_MODEL

kernel_name = "hybrid_gated_na_mla_swa_convffn"


def rmsnorm(x, g):
    xf = x.astype(jnp.float32)
    y = xf * lax.rsqrt(jnp.mean(xf * xf, axis=-1, keepdims=True) + EPS)
    return (y * g.astype(jnp.float32)).astype(x.dtype)


def rope_tables(seq, dim):
    inv = 1.0 / (ROPE_THETA ** (jnp.arange(0, dim, 2, dtype=jnp.float32) / dim))
    ang = jnp.arange(seq, dtype=jnp.float32)[:, None] * inv[None, :]
    return jnp.cos(ang), jnp.sin(ang)


def apply_rope(x, cos, sin):
    x1, x2 = jnp.split(x, 2, axis=-1)
    c = cos[None, :, None, :].astype(x.dtype)
    s = sin[None, :, None, :].astype(x.dtype)
    return jnp.concatenate([x1 * c - x2 * s, x1 * s + x2 * c], axis=-1)


def neighbourhood_attention(q, k, v, rpb):
    B, S, H, hd = q.shape
    rows = S // GRID_W
    win_r = min(NA_WIN_R, rows)
    qg = q.reshape(B, rows, GRID_W, H, hd)
    kg = k.reshape(B, rows, GRID_W, H, hd)
    vg = v.reshape(B, rows, GRID_W, H, hd)
    col_q = jnp.arange(GRID_W)
    col_start = jnp.clip(col_q - NA_WIN_C // 2, 0, GRID_W - NA_WIN_C)
    col_idx = col_start[:, None] + jnp.arange(NA_WIN_C)[None, :]
    col_bias_idx = col_idx - col_q[:, None] + (NA_WIN_C - 1)
    scale = hd ** -0.5

    def one_row(r):
        r_start = jnp.clip(r - NA_WIN_R // 2, 0, rows - win_r)
        q_r = lax.dynamic_index_in_dim(qg, r, axis=1, keepdims=False)
        k_rows = lax.dynamic_slice_in_dim(kg, r_start, win_r, axis=1)
        v_rows = lax.dynamic_slice_in_dim(vg, r_start, win_r, axis=1)
        k_win = k_rows[:, :, col_idx]
        v_win = v_rows[:, :, col_idx]
        s = jnp.einsum('bqhd,brqchd->bhqrc', q_r, k_win,
                       preferred_element_type=jnp.float32) * scale
        row_bias_idx = r_start + jnp.arange(win_r) - r + (NA_WIN_R - 1)
        bias = rpb[:, row_bias_idx[None, :, None], col_bias_idx[:, None, :]]
        s = s + bias[None].astype(jnp.float32)
        p = jax.nn.softmax(s.reshape(B, H, GRID_W, win_r * NA_WIN_C), axis=-1)
        p = p.reshape(B, H, GRID_W, win_r, NA_WIN_C).astype(v.dtype)
        return jnp.einsum('bhqrc,brqchd->bqhd', p, v_win)

    out = lax.map(one_row, jnp.arange(rows))
    return out.transpose(1, 0, 2, 3, 4).reshape(B, S, H * hd)


def mla_attention(c_q, c_kv, k_rope, g_qa, g_kva, w_uq, w_ukv, cos, sin):
    B, S, _ = c_q.shape
    H = MLA_HEADS
    q = (rmsnorm(c_q, g_qa) @ w_uq).reshape(B, S, H, MLA_QK)
    q_nope = q[..., :MLA_NOPE]
    q_pe = apply_rope(q[..., MLA_NOPE:], cos, sin)
    kv = (rmsnorm(c_kv, g_kva) @ w_ukv).reshape(B, S, H, MLA_NOPE + MLA_V)
    k_nope = kv[..., :MLA_NOPE]
    v = kv[..., MLA_NOPE:]
    k_pe = apply_rope(k_rope[:, :, None, :], cos, sin)[:, :, 0, :]
    scale = MLA_QK ** -0.5
    nb = S // BLOCK
    qn_b = q_nope.reshape(B, nb, BLOCK, H, MLA_NOPE).transpose(1, 0, 2, 3, 4)
    qp_b = q_pe.reshape(B, nb, BLOCK, H, MLA_ROPE).transpose(1, 0, 2, 3, 4)

    def one_block(args):
        qn, qp = args
        s = (jnp.einsum('bqhd,bkhd->bhqk', qn, k_nope, preferred_element_type=jnp.float32)
             + jnp.einsum('bqhr,bkr->bhqk', qp, k_pe, preferred_element_type=jnp.float32)) * scale
        p = jax.nn.softmax(s, axis=-1).astype(v.dtype)
        return jnp.einsum('bhqk,bkhd->bqhd', p, v)

    o = lax.map(one_block, (qn_b, qp_b))
    return o.transpose(1, 0, 2, 3, 4).reshape(B, S, H * MLA_V)


def window_gqa_sink(q, k, v, sink, cos, sin):
    B, S, H, hd = q.shape
    KVH = k.shape[2]
    G = H // KVH
    q = apply_rope(q, cos, sin)
    k = apply_rope(k, cos, sin)
    nb = S // BLOCK
    qb = q.reshape(B, nb, BLOCK, KVH, G, hd)
    pad = ((0, 0), (BLOCK, BLOCK), (0, 0), (0, 0))
    kp = jnp.pad(k, pad).reshape(B, nb + 2, BLOCK, KVH, hd)
    vp = jnp.pad(v, pad).reshape(B, nb + 2, BLOCK, KVH, hd)
    kw = jnp.concatenate([kp[:, :-2], kp[:, 1:-1], kp[:, 2:]], axis=2)
    vw = jnp.concatenate([vp[:, :-2], vp[:, 1:-1], vp[:, 2:]], axis=2)
    scale = hd ** -0.5
    s = jnp.einsum('bnqkgd,bnjkd->bnkgqj', qb, kw,
                   preferred_element_type=jnp.float32) * scale
    blk = jnp.arange(nb)
    qpos = blk[:, None] * BLOCK + jnp.arange(BLOCK)[None, :]
    kpos = (blk[:, None] - 1) * BLOCK + jnp.arange(3 * BLOCK)[None, :]
    valid = ((kpos[:, None, :] >= 0) & (kpos[:, None, :] < S)
             & (jnp.abs(qpos[:, :, None] - kpos[:, None, :]) <= GQA_WINDOW))
    s = jnp.where(valid[None, :, None, None], s, NEG_INF)
    sink_b = jnp.broadcast_to(sink.astype(jnp.float32).reshape(KVH, G)[None, None, :, :, None, None],
                              s.shape[:-1] + (1,))
    p = jax.nn.softmax(jnp.concatenate([s, sink_b], axis=-1), axis=-1)[..., :-1]
    o = jnp.einsum('bnkgqj,bnjkd->bnqkgd', p.astype(v.dtype), vw)
    return o.reshape(B, S, H * hd)


def gated_parallel_mixer(h, w_in, b_gate, na_rpb, mla_qa_g, mla_kva_g, mla_w_uq, mla_w_ukv,
                         gqa_sink, w_br_na, w_br_mla, w_br_gqa, w_out, cos_mla, sin_mla, cos_gqa, sin_gqa):
    B, S, _ = h.shape
    z = h @ w_in
    offs = np.cumsum(IN_SPLITS)[:-1].tolist()
    z_na, c_q, c_kv, k_rope, z_gq, z_gkv, z_gate = jnp.split(z, offs, axis=-1)
    na_qkv = z_na.reshape(B, S, 3, NA_HEADS, NA_HEAD_DIM)
    y_na = neighbourhood_attention(na_qkv[:, :, 0], na_qkv[:, :, 1], na_qkv[:, :, 2], na_rpb)
    y_mla = mla_attention(c_q, c_kv, k_rope, mla_qa_g, mla_kva_g, mla_w_uq, mla_w_ukv, cos_mla, sin_mla)
    gkv = z_gkv.reshape(B, S, 2, GQA_KV_HEADS, GQA_HEAD_DIM)
    y_gqa = window_gqa_sink(z_gq.reshape(B, S, GQA_HEADS, GQA_HEAD_DIM), gkv[:, :, 0], gkv[:, :, 1],
                            gqa_sink, cos_gqa, sin_gqa)
    gates = jax.nn.sigmoid(z_gate + b_gate).reshape(B, S, N_BRANCH, D_MODEL)
    merged = (gates[:, :, 0] * (y_na @ w_br_na)
              + gates[:, :, 1] * (y_mla @ w_br_mla)
              + gates[:, :, 2] * (y_gqa @ w_br_gqa))
    return merged @ w_out


def conv_ffn(h, w_up, conv_w, conv_b, w_down):
    u = h @ w_up
    C = u.shape[-1]
    u = lax.conv_general_dilated(u, conv_w[:, None, :].astype(u.dtype), window_strides=(1,),
                                 padding=((CONV_W // 2, CONV_W // 2),),
                                 dimension_numbers=('NWC', 'WIO', 'NWC'),
                                 feature_group_count=C) + conv_b
    a, g = jnp.split(u, 2, axis=-1)
    return (jax.nn.gelu(g) * a) @ w_down


def setup_inputs(seed: int = 0) -> dict:
    key = jax.random.key(seed)
    ks = jax.random.split(key, 24)
    f32 = jnp.float32

    def nrm(k, shape, scale):
        return jax.random.normal(k, shape, f32) * scale

    L = DEPTH
    return {
        "x": nrm(ks[0], (BATCH, SEQ, D_MODEL), 1.0),
        "norm1_g": 1.0 + nrm(ks[1], (L, D_MODEL), 0.05),
        "w_in": nrm(ks[2], (L, D_MODEL, IN_W), D_MODEL ** -0.5),
        "b_gate": nrm(ks[3], (L, N_BRANCH * D_MODEL), 0.02),
        "na_rpb": nrm(ks[4], (L, NA_HEADS, 2 * NA_WIN_R - 1, 2 * NA_WIN_C - 1), 0.1),
        "mla_qa_g": 1.0 + nrm(ks[5], (L, MLA_Q_RANK), 0.05),
        "mla_kva_g": 1.0 + nrm(ks[6], (L, MLA_KV_RANK), 0.05),
        "mla_w_uq": nrm(ks[7], (L, MLA_Q_RANK, MLA_HEADS * MLA_QK), MLA_Q_RANK ** -0.5),
        "mla_w_ukv": nrm(ks[8], (L, MLA_KV_RANK, MLA_HEADS * (MLA_NOPE + MLA_V)), MLA_KV_RANK ** -0.5),
        "gqa_sink": nrm(ks[9], (L, GQA_HEADS), 0.5),
        "w_br_na": nrm(ks[10], (L, NA_W, D_MODEL), NA_W ** -0.5),
        "w_br_mla": nrm(ks[11], (L, MLA_W, D_MODEL), MLA_W ** -0.5),
        "w_br_gqa": nrm(ks[12], (L, GQA_Q_W, D_MODEL), GQA_Q_W ** -0.5),
        "w_out": nrm(ks[13], (L, D_MODEL, D_MODEL), D_MODEL ** -0.5),
        "norm2_g": 1.0 + nrm(ks[14], (L, D_MODEL), 0.05),
        "w_up": nrm(ks[15], (L, D_MODEL, 2 * D_FF), D_MODEL ** -0.5),
        "conv_w": nrm(ks[16], (L, CONV_W, 2 * D_FF), CONV_W ** -0.5),
        "conv_b": nrm(ks[17], (L, 2 * D_FF), 0.02),
        "w_down": nrm(ks[18], (L, D_FF, D_MODEL), D_FF ** -0.5),
        "final_g": 1.0 + nrm(ks[19], (D_MODEL,), 0.05),
    }


def reference(x, norm1_g, w_in, b_gate, na_rpb, mla_qa_g, mla_kva_g, mla_w_uq, mla_w_ukv,
              gqa_sink, w_br_na, w_br_mla, w_br_gqa, w_out, norm2_g, w_up, conv_w, conv_b,
              w_down, final_g):
    S = x.shape[1]
    cos_mla, sin_mla = rope_tables(S, MLA_ROPE)
    cos_gqa, sin_gqa = rope_tables(S, GQA_HEAD_DIM)
    for l in range(DEPTH):
        h = rmsnorm(x, norm1_g[l])
        x = x + gated_parallel_mixer(h, w_in[l], b_gate[l], na_rpb[l], mla_qa_g[l], mla_kva_g[l],
                                     mla_w_uq[l], mla_w_ukv[l], gqa_sink[l], w_br_na[l], w_br_mla[l],
                                     w_br_gqa[l], w_out[l], cos_mla, sin_mla, cos_gqa, sin_gqa)
        h = rmsnorm(x, norm2_g[l])
        x = x + conv_ffn(h, w_up[l], conv_w[l], conv_b[l], w_down[l])
    return rmsnorm(x, final_g)
```

```python
import functools

import jax
import jax.numpy as jnp
import numpy as np
from jax import lax
from jax.experimental import pallas as pl
from jax.experimental.pallas import tpu as pltpu

F32 = jnp.float32
BF16 = jnp.bfloat16

D_MODEL = 1024
GRID_W = 64
NA_HEADS = 8
NA_HEAD_DIM = 64
NA_WIN_R = 8
NA_WIN_C = 16
NA_W = NA_HEADS * NA_HEAD_DIM
MLA_HEADS = 4
MLA_Q_RANK = 384
MLA_KV_RANK = 256
MLA_NOPE = 128
MLA_ROPE = 64
MLA_V = 128
MLA_QK = MLA_NOPE + MLA_ROPE
MLA_QK_PAD = 256
GQA_HEADS = 8
GQA_KV_HEADS = 2
GQA_GROUP = GQA_HEADS // GQA_KV_HEADS
GQA_HEAD_DIM = 64
GQA_WINDOW = 128
GQA_Q_W = GQA_HEADS * GQA_HEAD_DIM
GQA_KV_W = GQA_KV_HEADS * GQA_HEAD_DIM
BLOCK = 128
N_BRANCH = 3
D_FF = 2816
ROPE_THETA = 10000.0
EPS = 1e-6
NEG_INF = -1e30

LANES = 128
IN1_W = 3072
FF_CHUNK = 256
N_FF_CHUNKS = D_FF // FF_CHUNK
HALO = 16
VMEM_LIMIT = 56 * 1024 * 1024

_O_NAQ, _O_NAK, _O_NAV = 0, 512, 1024
_O_CQ = 1536
_O_CKV = 1920
_O_GQ = 2176
_O_GK = 2688
_O_GV = 2816
_O_KR = 2944


def _rms(x, g):
    return x * lax.rsqrt(jnp.mean(x * x, axis=-1, keepdims=True) + EPS) * g


def _rope128(blk, c, s1, s2):
    return blk * c + pltpu.roll(blk, 32, 1) * s1 + pltpu.roll(blk, 96, 1) * s2


def _dot(a, b):
    return jnp.dot(a, b, preferred_element_type=F32)


def _dot_nt(a, b):
    return lax.dot_general(a, b, (((1,), (1,)), ((), ())), preferred_element_type=F32)


def _params(n_axes):
    return pltpu.CompilerParams(dimension_semantics=("arbitrary",) * n_axes,
                                vmem_limit_bytes=VMEM_LIMIT)


def _const_spec(shape):
    n = len(shape)
    return pl.BlockSpec(shape, lambda *_: (0,) * n, pipeline_mode=pl.Buffered(1))


def _inproj_kernel(x_ref, g1_ref, w1_ref, gqa_ref, gkva_ref, wuq_ref, wukv_ref, rope_ref,
                   na_ref, mq_ref, mk_ref, mv_ref, gq_ref, gkv_ref):
    h = _rms(x_ref[...], g1_ref[...]).astype(BF16)
    z = _dot(h, w1_ref[...])
    na_ref[...] = z[:, _O_NAQ:_O_CQ].astype(BF16)

    mc, ms1, ms2 = rope_ref[:, 0:128], rope_ref[:, 128:256], rope_ref[:, 256:384]
    gc, gs1, gs2 = rope_ref[:, 384:512], rope_ref[:, 512:640], rope_ref[:, 640:768]

    cqn = _rms(z[:, _O_CQ:_O_CKV], gqa_ref[...]).astype(BF16)
    q = _dot(cqn, wuq_ref[...])
    ckvn = _rms(z[:, _O_CKV:_O_GQ], gkva_ref[...]).astype(BF16)
    kv = _dot(ckvn, wukv_ref[...])
    kpe = _rope128(z[:, _O_KR:_O_KR + LANES], mc, ms1, ms2).astype(BF16)
    for hd in range(MLA_HEADS):
        o = hd * MLA_QK_PAD
        mq_ref[:, o:o + LANES] = q[:, o:o + LANES].astype(BF16)
        mq_ref[:, o + LANES:o + 2 * LANES] = _rope128(
            q[:, o + LANES:o + 2 * LANES], mc, ms1, ms2).astype(BF16)
        mk_ref[:, o:o + LANES] = kv[:, hd * LANES:(hd + 1) * LANES].astype(BF16)
        mk_ref[:, o + LANES:o + 2 * LANES] = kpe
    mv_ref[...] = kv[:, MLA_HEADS * MLA_NOPE:].astype(BF16)

    for t in range(GQA_Q_W // LANES):
        o = _O_GQ + t * LANES
        gq_ref[:, t * LANES:(t + 1) * LANES] = _rope128(z[:, o:o + LANES], gc, gs1, gs2).astype(BF16)
    gkv_ref[:, 0:LANES] = _rope128(z[:, _O_GK:_O_GV], gc, gs1, gs2).astype(BF16)
    gkv_ref[:, LANES:2 * LANES] = z[:, _O_GV:_O_KR].astype(BF16)


def _inproj(x2d, g1, w1, g_qa, g_kva, wuq, wukv, rope, *, seq, tm):
    T = x2d.shape[0]
    tiles_per_seq = seq // tm
    row = lambda w: pl.BlockSpec((tm, w), lambda i: (i, 0))
    out_w = (3 * NA_W, MLA_HEADS * MLA_QK_PAD, MLA_HEADS * MLA_QK_PAD, MLA_HEADS * MLA_V,
             GQA_Q_W, 2 * GQA_KV_W)
    return pl.pallas_call(
        _inproj_kernel,
        grid=(T // tm,),
        in_specs=[row(D_MODEL), _const_spec((1, D_MODEL)), _const_spec((D_MODEL, IN1_W)),
                  _const_spec((1, MLA_Q_RANK)), _const_spec((1, MLA_KV_RANK)),
                  _const_spec((MLA_Q_RANK, MLA_HEADS * MLA_QK_PAD)),
                  _const_spec((MLA_KV_RANK, MLA_HEADS * (MLA_NOPE + MLA_V))),
                  pl.BlockSpec((tm, 6 * LANES), lambda i: (i % tiles_per_seq, 0))],
        out_specs=[row(w) for w in out_w],
        out_shape=[jax.ShapeDtypeStruct((T, w), BF16) for w in out_w],
        compiler_params=_params(1),
        name="inproj",
    )(x2d, g1, w1, g_qa, g_kva, wuq, wukv, rope)


NA_ROWS_PER_STEP = 8


def _na_kernel(q_ref, k_ref, v_ref, tb_ref, o_ref, *, rows):
    rb = pl.program_id(1)
    lane = lax.broadcasted_iota(jnp.int32, (GRID_W, LANES), 1)
    left = lane < NA_HEAD_DIM
    n_keys = NA_WIN_R * GRID_W

    def row_body(rr, carry):
        r = rb * NA_ROWS_PER_STEP + rr
        r_start = jnp.clip(r - NA_WIN_R // 2, 0, rows - NA_WIN_R)
        d = r_start - r + (NA_WIN_R - 1)
        tok0 = pl.multiple_of(r_start * GRID_W, GRID_W)
        q0 = pl.multiple_of(rr * GRID_W, GRID_W)
        for hp in range(NA_HEADS // 2):
            cols = slice(hp * LANES, (hp + 1) * LANES)
            q2 = q_ref[pl.ds(q0, GRID_W), cols]
            k2 = k_ref[pl.ds(tok0, n_keys), cols]
            v2 = v_ref[pl.ds(tok0, n_keys), cols]
            outs = []
            for half in range(2):
                hd = 2 * hp + half
                keep = left if half == 0 else jnp.logical_not(left)
                qm = jnp.where(keep, q2, jnp.zeros_like(q2))
                s = _dot_nt(qm, k2)
                bias = jnp.concatenate([tb_ref[hd, d + 2 * j] for j in range(NA_WIN_R // 2)], axis=1)
                s = s + bias
                m = jnp.max(s, axis=-1, keepdims=True)
                p = jnp.exp(s - m)
                l = jnp.sum(p, axis=-1, keepdims=True)
                outs.append(_dot(p.astype(BF16), v2) / l)
            o_ref[pl.ds(q0, GRID_W), cols] = jnp.where(left, outs[0], outs[1]).astype(BF16)
        return carry

    lax.fori_loop(0, NA_ROWS_PER_STEP, row_body, 0)


def _na(na_qkv, tb, *, batch, seq):
    rows = seq // GRID_W
    tq = NA_ROWS_PER_STEP * GRID_W
    qkv3 = na_qkv.reshape(batch, seq, 3 * NA_W)
    full = lambda c: pl.BlockSpec((None, seq, NA_W), lambda b, i: (b, 0, c),
                                  pipeline_mode=pl.Buffered(1))
    return pl.pallas_call(
        functools.partial(_na_kernel, rows=rows),
        grid=(batch, seq // tq),
        in_specs=[pl.BlockSpec((None, tq, NA_W), lambda b, i: (b, i, 0)), full(1), full(2),
                  _const_spec(tb.shape)],
        out_specs=pl.BlockSpec((None, tq, NA_W), lambda b, i: (b, i, 0)),
        out_shape=jax.ShapeDtypeStruct((batch, seq, NA_W), BF16),
        compiler_params=_params(2),
        name="na",
    )(qkv3, qkv3, qkv3, tb)


def _na_bias_table(rpb):
    c = np.arange(GRID_W)
    c_start = np.clip(c - NA_WIN_C // 2, 0, GRID_W - NA_WIN_C)
    kc = np.arange(GRID_W)
    valid = (kc[None, :] >= c_start[:, None]) & (kc[None, :] < c_start[:, None] + NA_WIN_C)
    rel = np.clip(kc[None, :] - c[:, None] + (NA_WIN_C - 1), 0, 2 * NA_WIN_C - 2)
    t2 = jnp.where(valid[None, None], rpb[:, :, rel], NEG_INF)
    return jnp.concatenate([t2[:, :-1], t2[:, 1:]], axis=-1).astype(F32)


def _mla_kernel(q_ref, k_ref, v_ref, o_ref, m_sc, l_sc, acc_sc, *, scale):
    kv = pl.program_id(3)

    @pl.when(kv == 0)
    def _():
        m_sc[...] = jnp.full(m_sc.shape, -jnp.inf, F32)
        l_sc[...] = jnp.zeros(l_sc.shape, F32)
        acc_sc[...] = jnp.zeros(acc_sc.shape, F32)

    s = _dot_nt(q_ref[...], k_ref[...]) * scale
    m_prev = m_sc[...]
    m_new = jnp.maximum(m_prev, jnp.max(s, axis=-1, keepdims=True))
    alpha = jnp.exp(m_prev - m_new)
    p = jnp.exp(s - m_new)
    l_sc[...] = alpha * l_sc[...] + jnp.sum(p, axis=-1, keepdims=True)
    acc_sc[...] = alpha * acc_sc[...] + _dot(p.astype(BF16), v_ref[...])
    m_sc[...] = m_new

    @pl.when(kv == pl.num_programs(3) - 1)
    def _():
        o_ref[...] = (acc_sc[...] / l_sc[...]).astype(BF16)


def _mla(mq, mk, mv, *, batch, seq, tq, tk):
    q3 = mq.reshape(batch, seq, MLA_HEADS * MLA_QK_PAD)
    k3 = mk.reshape(batch, seq, MLA_HEADS * MLA_QK_PAD)
    v3 = mv.reshape(batch, seq, MLA_HEADS * MLA_V)
    return pl.pallas_call(
        functools.partial(_mla_kernel, scale=MLA_QK ** -0.5),
        grid=(batch, MLA_HEADS, seq // tq, seq // tk),
        in_specs=[pl.BlockSpec((None, tq, MLA_QK_PAD), lambda b, h, i, j: (b, i, h)),
                  pl.BlockSpec((None, tk, MLA_QK_PAD), lambda b, h, i, j: (b, j, h)),
                  pl.BlockSpec((None, tk, MLA_V), lambda b, h, i, j: (b, j, h))],
        out_specs=pl.BlockSpec((None, tq, MLA_V), lambda b, h, i, j: (b, i, h)),
        out_shape=jax.ShapeDtypeStruct((batch, seq, MLA_HEADS * MLA_V), BF16),
        scratch_shapes=[pltpu.VMEM((tq, 1), F32), pltpu.VMEM((tq, 1), F32),
                        pltpu.VMEM((tq, MLA_V), F32)],
        compiler_params=_params(4),
        name="mla",
    )(q3, k3, v3)


def _gqa_kernel(q_ref, kp_ref, kc_ref, kn_ref, vp_ref, vc_ref, vn_ref, b_ref, sk_ref, o_ref):
    lane = lax.broadcasted_iota(jnp.int32, (BLOCK, LANES), 1)
    left = lane < GQA_HEAD_DIM
    k2 = jnp.concatenate([kp_ref[...], kc_ref[...], kn_ref[...]], axis=0)
    v2 = jnp.concatenate([vp_ref[...], vc_ref[...], vn_ref[...]], axis=0)
    bias = b_ref[...]
    for t in range(GQA_Q_W // LANES):
        qt = q_ref[:, t * LANES:(t + 1) * LANES]
        outs = []
        for kvh in range(GQA_KV_HEADS):
            hd = kvh * GQA_GROUP + t
            keep = left if kvh == 0 else jnp.logical_not(left)
            qm = jnp.where(keep, qt, jnp.zeros_like(qt))
            s = _dot_nt(qm, k2) + bias
            sink = sk_ref[hd:hd + 1, 0:1]
            m = jnp.maximum(jnp.max(s, axis=-1, keepdims=True), sink)
            p = jnp.exp(s - m)
            l = jnp.sum(p, axis=-1, keepdims=True) + jnp.exp(sink - m)
            outs.append(_dot(p.astype(BF16), v2) / l)
        o_ref[:, t * LANES:(t + 1) * LANES] = jnp.where(left, outs[0], outs[1]).astype(BF16)


def _gqa(gq, gkv, bias3, sink, *, batch, seq):
    nb = seq // BLOCK
    q3 = gq.reshape(batch, seq, GQA_Q_W)
    kv3 = gkv.reshape(batch, seq, 2 * GQA_KV_W)
    blk = lambda c, f: pl.BlockSpec((None, BLOCK, LANES), lambda b, i: (b, f(i), c))
    prev = lambda i: jnp.maximum(i - 1, 0)
    cur = lambda i: i
    nxt = lambda i: jnp.minimum(i + 1, nb - 1)
    which = lambda b, i: (jnp.where(i == 0, 0, jnp.where(i == nb - 1, 2, 1)), 0, 0)
    return pl.pallas_call(
        _gqa_kernel,
        grid=(batch, nb),
        in_specs=[pl.BlockSpec((None, BLOCK, GQA_Q_W), lambda b, i: (b, i, 0)),
                  blk(0, prev), blk(0, cur), blk(0, nxt), blk(1, prev), blk(1, cur), blk(1, nxt),
                  pl.BlockSpec((None, BLOCK, 3 * BLOCK), which),
                  _const_spec((GQA_HEADS, LANES))],
        out_specs=pl.BlockSpec((None, BLOCK, GQA_Q_W), lambda b, i: (b, i, 0)),
        out_shape=jax.ShapeDtypeStruct((batch, seq, GQA_Q_W), BF16),
        compiler_params=_params(2),
        name="gqa",
    )(q3, kv3, kv3, kv3, kv3, kv3, kv3, bias3, sink)


def _gqa_mask_bias(nb):
    i = np.arange(BLOCK)[:, None]
    j = np.arange(3 * BLOCK)[None, :]
    band = np.abs(BLOCK + i - j) <= GQA_WINDOW
    first = band & (j >= BLOCK)
    last = band & (j < 2 * BLOCK)
    if nb == 1:
        first = last = first & last
    m = np.stack([first, band, last])
    return jnp.asarray(np.where(m, 0.0, NEG_INF), F32)


def _mixout_kernel(x_ref, yna_ref, ymla_ref, ygqa_ref, g1_ref, wg_ref, bg_ref, wbr_ref, wo_ref, o_ref):
    x = x_ref[...]
    h = _rms(x, g1_ref[...]).astype(BF16)
    merged = None
    for i, y_ref in enumerate((yna_ref, ymla_ref, ygqa_ref)):
        cols = slice(i * D_MODEL, (i + 1) * D_MODEL)
        gate = jax.nn.sigmoid(_dot(h, wg_ref[:, cols]) + bg_ref[:, cols])
        term = gate * _dot(y_ref[...], wbr_ref[i])
        merged = term if merged is None else merged + term
    o_ref[...] = x + _dot(merged.astype(BF16), wo_ref[...])


def _mixout(x2d, y_na, y_mla, y_gqa, g1, wg, bg, wbr, wo, *, tm):
    T = x2d.shape[0]
    row = lambda w: pl.BlockSpec((tm, w), lambda i: (i, 0))
    return pl.pallas_call(
        _mixout_kernel,
        grid=(T // tm,),
        in_specs=[row(D_MODEL), row(NA_W), row(MLA_HEADS * MLA_V), row(GQA_Q_W),
                  _const_spec((1, D_MODEL)), _const_spec((D_MODEL, N_BRANCH * D_MODEL)),
                  _const_spec((1, N_BRANCH * D_MODEL)), _const_spec((N_BRANCH, NA_W, D_MODEL)),
                  _const_spec((D_MODEL, D_MODEL))],
        out_specs=row(D_MODEL),
        out_shape=jax.ShapeDtypeStruct((T, D_MODEL), F32),
        compiler_params=_params(1),
        name="mixout",
    )(x2d, y_na, y_mla, y_gqa, g1, wg, bg, wbr, wo)


def _ffn_kernel(x_ref, xp_ref, xn_ref, g2_ref, wup_ref, cw_ref, cb_ref, wd_ref, fg_ref, o_ref,
                hx_ref, u_ref, acc_ref, *, tm, tiles_per_seq, final):
    j = pl.program_id(0) % tiles_per_seq
    g2 = g2_ref[...]
    x = x_ref[...]
    hp = jnp.where(j == 0, 0.0, _rms(xp_ref[...], g2))
    hn = jnp.where(j == tiles_per_seq - 1, 0.0, _rms(xn_ref[...], g2))
    hx_ref[0:HALO, :] = hp.astype(BF16)
    hx_ref[HALO:HALO + tm, :] = _rms(x, g2).astype(BF16)
    hx_ref[HALO + tm:, :] = hn.astype(BF16)
    acc_ref[...] = jnp.zeros(acc_ref.shape, F32)

    def chunk(c, carry):
        u_ref[...] = _dot(hx_ref[...], wup_ref[c])
        cw = cw_ref[c]
        y = (cw[0:1] * u_ref[pl.ds(HALO - 1, tm), :] + cw[1:2] * u_ref[pl.ds(HALO, tm), :]
             + cw[2:3] * u_ref[pl.ds(HALO + 1, tm), :] + cb_ref[c])
        act = jax.nn.gelu(y[:, FF_CHUNK:]) * y[:, :FF_CHUNK]
        acc_ref[...] += _dot(act.astype(BF16), wd_ref[c])
        return carry

    lax.fori_loop(0, N_FF_CHUNKS, chunk, 0)
    out = x + acc_ref[...]
    if final:
        out = _rms(out, fg_ref[...])
    o_ref[...] = out


def _ffn(x2d, g2, wup, cw, cb, wd, fg, *, seq, tm, final):
    T = x2d.shape[0]
    tiles_per_seq = seq // tm
    hb = tm // HALO
    n_hb = T // HALO
    return pl.pallas_call(
        functools.partial(_ffn_kernel, tm=tm, tiles_per_seq=tiles_per_seq, final=final),
        grid=(T // tm,),
        in_specs=[pl.BlockSpec((tm, D_MODEL), lambda i: (i, 0)),
                  pl.BlockSpec((HALO, D_MODEL), lambda i: (jnp.maximum(i * hb - 1, 0), 0)),
                  pl.BlockSpec((HALO, D_MODEL), lambda i: (jnp.minimum((i + 1) * hb, n_hb - 1), 0)),
                  _const_spec((1, D_MODEL)),
                  _const_spec((N_FF_CHUNKS, D_MODEL, 2 * FF_CHUNK)),
                  _const_spec((N_FF_CHUNKS, 3, 2 * FF_CHUNK)),
                  _const_spec((N_FF_CHUNKS, 1, 2 * FF_CHUNK)),
                  _const_spec((N_FF_CHUNKS, FF_CHUNK, D_MODEL)),
                  _const_spec((1, D_MODEL))],
        out_specs=pl.BlockSpec((tm, D_MODEL), lambda i: (i, 0)),
        out_shape=jax.ShapeDtypeStruct((T, D_MODEL), F32),
        scratch_shapes=[pltpu.VMEM((tm + 2 * HALO, D_MODEL), BF16),
                        pltpu.VMEM((tm + 2 * HALO, 2 * FF_CHUNK), F32),
                        pltpu.VMEM((tm, D_MODEL), F32)],
        compiler_params=_params(1),
        name="ffn",
    )(x2d, x2d, x2d, g2, wup, cw, cb, wd, fg)


def _rope_tables(seq):
    def cs(dim):
        inv = 1.0 / (ROPE_THETA ** (jnp.arange(0, dim, 2, dtype=F32) / dim))
        ang = jnp.arange(seq, dtype=F32)[:, None] * inv[None, :]
        return jnp.cos(ang), jnp.sin(ang)

    def head(c, s):
        z = jnp.zeros_like(s)
        return (jnp.concatenate([c, c], -1), jnp.concatenate([z, s], -1), jnp.concatenate([-s, z], -1))

    mc, ms1, ms2 = head(*cs(MLA_ROPE))
    pad = jnp.zeros_like(mc)
    gc, gs1, gs2 = head(*cs(GQA_HEAD_DIM))
    two = lambda a: jnp.concatenate([a, a], -1)
    return jnp.concatenate([mc, pad, ms1, pad, ms2, pad, two(gc), two(gs1), two(gs2)], axis=-1)


def _gqa_perm():
    p = []
    for t in range(GQA_GROUP):
        for kvh in range(GQA_KV_HEADS):
            hd = kvh * GQA_GROUP + t
            p.extend(range(hd * GQA_HEAD_DIM, (hd + 1) * GQA_HEAD_DIM))
    return np.asarray(p)


def _prep_layer(w_in, b_gate, mla_w_uq, mla_w_ukv, w_br_na, w_br_mla, w_br_gqa, w_out,
                w_up, conv_w, conv_b, w_down):
    o_cq = 3 * NA_W
    o_ckv = o_cq + MLA_Q_RANK
    o_kr = o_ckv + MLA_KV_RANK
    o_gq = o_kr + MLA_ROPE
    o_gkv = o_gq + GQA_Q_W
    o_gate = o_gkv + 2 * GQA_KV_W
    perm = _gqa_perm()
    na_scale = NA_HEAD_DIM ** -0.5
    gqa_scale = GQA_HEAD_DIM ** -0.5
    w1 = jnp.concatenate([
        w_in[:, 0:NA_W] * na_scale, w_in[:, NA_W:o_cq],
        w_in[:, o_cq:o_ckv], w_in[:, o_ckv:o_kr],
        w_in[:, o_gq:o_gkv][:, perm] * gqa_scale, w_in[:, o_gkv:o_gate],
        w_in[:, o_kr:o_gq], jnp.zeros((D_MODEL, LANES - MLA_ROPE), w_in.dtype)], axis=1).astype(BF16)
    wg = w_in[:, o_gate:].astype(BF16)
    wuq = jnp.pad(mla_w_uq.reshape(MLA_Q_RANK, MLA_HEADS, MLA_QK),
                  ((0, 0), (0, 0), (0, MLA_QK_PAD - MLA_QK))).reshape(MLA_Q_RANK, -1).astype(BF16)
    ukv = mla_w_ukv.reshape(MLA_KV_RANK, MLA_HEADS, MLA_NOPE + MLA_V)
    wukv = jnp.concatenate([ukv[:, :, :MLA_NOPE].reshape(MLA_KV_RANK, -1),
                            ukv[:, :, MLA_NOPE:].reshape(MLA_KV_RANK, -1)], axis=1).astype(BF16)
    wbr = jnp.stack([w_br_na, w_br_mla, w_br_gqa[perm]]).astype(BF16)
    chunks = lambda a: a.reshape(a.shape[0], 2, N_FF_CHUNKS, FF_CHUNK)
    wup = chunks(w_up).transpose(2, 0, 1, 3).reshape(N_FF_CHUNKS, D_MODEL, 2 * FF_CHUNK).astype(BF16)
    cw = chunks(conv_w).transpose(2, 0, 1, 3).reshape(N_FF_CHUNKS, 3, 2 * FF_CHUNK)
    cb = chunks(conv_b[None]).transpose(2, 0, 1, 3).reshape(N_FF_CHUNKS, 1, 2 * FF_CHUNK)
    wd = w_down.reshape(N_FF_CHUNKS, FF_CHUNK, D_MODEL).astype(BF16)
    return dict(w1=w1, wg=wg, bg=b_gate[None], wuq=wuq, wukv=wukv, wbr=wbr, wo=w_out.astype(BF16),
                wup=wup, cw=cw, cb=cb, wd=wd)


def _tiles(seq):
    tm = min(512, seq)
    tq = min(1024, seq)
    return tm, tq


def kernel(x, norm1_g, w_in, b_gate, na_rpb, mla_qa_g, mla_kva_g, mla_w_uq, mla_w_ukv, gqa_sink, w_br_na, w_br_mla, w_br_gqa, w_out, norm2_g, w_up, conv_w, conv_b, w_down, final_g):
    batch, seq, _ = x.shape
    depth = w_in.shape[0]
    assert seq % (NA_ROWS_PER_STEP * GRID_W) == 0 and seq // GRID_W >= NA_WIN_R
    tm, tq = _tiles(seq)
    rope = _rope_tables(seq)
    gqa_bias = _gqa_mask_bias(seq // BLOCK)
    xf = x.reshape(batch * seq, D_MODEL)
    for l in range(depth):
        p = _prep_layer(w_in[l], b_gate[l], mla_w_uq[l], mla_w_ukv[l], w_br_na[l], w_br_mla[l],
                        w_br_gqa[l], w_out[l], w_up[l], conv_w[l], conv_b[l], w_down[l])
        g1 = norm1_g[l][None]
        na_qkv, mq, mk, mv, gq, gkv = _inproj(xf, g1, p["w1"], mla_qa_g[l][None], mla_kva_g[l][None],
                                              p["wuq"], p["wukv"], rope, seq=seq, tm=tm)
        y_na = _na(na_qkv, _na_bias_table(na_rpb[l]), batch=batch, seq=seq)
        y_mla = _mla(mq, mk, mv, batch=batch, seq=seq, tq=tq, tk=tq)
        sink = jnp.broadcast_to(gqa_sink[l].astype(F32)[:, None], (GQA_HEADS, LANES))
        y_gqa = _gqa(gq, gkv, gqa_bias, sink, batch=batch, seq=seq)
        flat = lambda y: y.reshape(batch * seq, -1)
        xf = _mixout(xf, flat(y_na), flat(y_mla), flat(y_gqa), g1, p["wg"], p["bg"], p["wbr"], p["wo"],
                     tm=tm)
        xf = _ffn(xf, norm2_g[l][None], p["wup"], p["cw"], p["cb"], p["wd"], final_g[None],
                  seq=seq, tm=tm, final=(l == depth - 1))
    return xf.reshape(batch, seq, D_MODEL)
```

```python
import functools

import jax
import jax.numpy as jnp
import numpy as np
from jax import lax
from jax.experimental import pallas as pl
from jax.experimental.pallas import tpu as pltpu

F32 = jnp.float32
BF16 = jnp.bfloat16

D_MODEL = 1024
GRID_W = 64
NA_HEADS = 8
NA_HEAD_DIM = 64
NA_WIN_R = 8
NA_WIN_C = 16
NA_W = NA_HEADS * NA_HEAD_DIM
MLA_HEADS = 4
MLA_Q_RANK = 384
MLA_KV_RANK = 256
MLA_NOPE = 128
MLA_ROPE = 64
MLA_V = 128
MLA_QK = MLA_NOPE + MLA_ROPE
MLA_QK_PAD = 256
MLA_KC = 256
MLA_VT_ROWS = MLA_V + 16
MLA_UNROLL = 4
GQA_HEADS = 8
GQA_KV_HEADS = 2
GQA_GROUP = GQA_HEADS // GQA_KV_HEADS
GQA_HEAD_DIM = 64
GQA_WINDOW = 128
GQA_Q_W = GQA_HEADS * GQA_HEAD_DIM
GQA_KV_W = GQA_KV_HEADS * GQA_HEAD_DIM
BLOCK = 128
N_BRANCH = 3
D_FF = 2816
ROPE_THETA = 10000.0
EPS = 1e-6
NEG_INF = -1e30

LANES = 128
IN1_W = 2432
FF_CHUNK = 256
N_FF_CHUNKS = D_FF // FF_CHUNK
HALO = 16
VMEM_LIMIT = 56 * 1024 * 1024

_O_NAQ, _O_NAK, _O_NAV = 0, 512, 1024
_O_CQ = 1536
_O_CKV = 1920
_O_GK = 2176
_O_KR = 2304


def _rms(x, g):
    return x * lax.rsqrt(jnp.mean(x * x, axis=-1, keepdims=True) + EPS) * g


def _rope128(blk, c, s1, s2):
    return blk * c + pltpu.roll(blk, 32, 1) * s1 + pltpu.roll(blk, 96, 1) * s2


def _dot(a, b):
    return jnp.dot(a, b, preferred_element_type=F32)


def _dot_nt(a, b):
    return lax.dot_general(a, b, (((1,), (1,)), ((), ())), preferred_element_type=F32)


def _params(n_axes):
    return pltpu.CompilerParams(dimension_semantics=("arbitrary",) * n_axes,
                                vmem_limit_bytes=VMEM_LIMIT)


def _const_spec(shape):
    n = len(shape)
    return pl.BlockSpec(shape, lambda *_: (0,) * n, pipeline_mode=pl.Buffered(1))


def _inproj_kernel(x_ref, g1_ref, w1_ref, gqa_ref, gkva_ref, wuqt_ref, wuk_ref, wuvt_ref, wgt_ref,
                   rope_ref, ropet_ref, na_ref, mqt_ref, mk_ref, mvt_ref, gqt_ref, gk_ref, gvt_ref):
    tm = x_ref.shape[0]
    h = _rms(x_ref[...], g1_ref[...]).astype(BF16)
    z = _dot(h, w1_ref[...])
    na_ref[...] = z[:, _O_NAQ:_O_CQ].astype(BF16)

    mc, ms1, ms2 = rope_ref[:, 0:128], rope_ref[:, 128:256], rope_ref[:, 256:384]
    gc, gs1, gs2 = rope_ref[:, 384:512], rope_ref[:, 512:640], rope_ref[:, 640:768]

    cqn = _rms(z[:, _O_CQ:_O_CKV], gqa_ref[...]).astype(BF16)
    qt = _dot_nt(wuqt_ref[...], cqn)
    half = MLA_ROPE // 2
    ct, st = ropet_ref[0:half, :], ropet_ref[half:MLA_ROPE, :]
    for hd in range(MLA_HEADS):
        o = hd * MLA_QK_PAD
        r = o + MLA_NOPE
        mqt_ref[o:r, :] = qt[o:r, :].astype(BF16)
        x1, x2 = qt[r:r + half, :], qt[r + half:r + MLA_ROPE, :]
        mqt_ref[r:r + half, :] = (x1 * ct - x2 * st).astype(BF16)
        mqt_ref[r + half:r + MLA_ROPE, :] = (x1 * st + x2 * ct).astype(BF16)
        mqt_ref[r + MLA_ROPE:o + MLA_QK_PAD, :] = qt[r + MLA_ROPE:o + MLA_QK_PAD, :].astype(BF16)

    ckvn = _rms(z[:, _O_CKV:_O_GK], gkva_ref[...]).astype(BF16)
    kn = _dot(ckvn, wuk_ref[...])
    kpe = _rope128(z[:, _O_KR:_O_KR + LANES], mc, ms1, ms2).astype(BF16)
    for hd in range(MLA_HEADS):
        o = hd * MLA_QK_PAD
        mk_ref[:, o:o + LANES] = kn[:, hd * LANES:(hd + 1) * LANES].astype(BF16)
        mk_ref[:, o + LANES:o + 2 * LANES] = kpe
    vt = _dot_nt(wuvt_ref[...], ckvn)
    ones_rows = (lax.broadcasted_iota(jnp.int32, (MLA_VT_ROWS - MLA_V, MLA_KC), 0) == 0).astype(BF16)
    for hd in range(MLA_HEADS):
        for c in range(tm // MLA_KC):
            mvt_ref[hd, c, 0:MLA_V, :] = vt[hd * MLA_V:(hd + 1) * MLA_V,
                                            c * MLA_KC:(c + 1) * MLA_KC].astype(BF16)
            mvt_ref[hd, c, MLA_V:MLA_VT_ROWS, :] = ones_rows

    gt = _dot_nt(wgt_ref[...], h)
    for hd in range(GQA_HEADS):
        r = hd * GQA_HEAD_DIM
        x1, x2 = gt[r:r + half, :], gt[r + half:r + GQA_HEAD_DIM, :]
        gqt_ref[r:r + half, :] = (x1 * ct - x2 * st).astype(BF16)
        gqt_ref[r + half:r + GQA_HEAD_DIM, :] = (x1 * st + x2 * ct).astype(BF16)
    gvt_ref[...] = gt[GQA_Q_W:, :].astype(BF16)
    gk_ref[...] = _rope128(z[:, _O_GK:_O_KR], gc, gs1, gs2).astype(BF16)


def _inproj(x2d, g1, w1, g_qa, g_kva, wuqt, wuk, wuvt, wgt, rope, ropet, *, seq, tm):
    T = x2d.shape[0]
    tiles_per_seq = seq // tm
    cpt = tm // MLA_KC
    row = lambda w: pl.BlockSpec((tm, w), lambda i: (i, 0))
    qk_w = MLA_HEADS * MLA_QK_PAD
    out_specs = [row(3 * NA_W),
                 pl.BlockSpec((qk_w, tm), lambda i: (0, i)),
                 row(qk_w),
                 pl.BlockSpec((MLA_HEADS, cpt, MLA_VT_ROWS, MLA_KC), lambda i: (0, i, 0, 0)),
                 pl.BlockSpec((GQA_Q_W, tm), lambda i: (0, i)),
                 row(GQA_KV_W),
                 pl.BlockSpec((GQA_KV_W, tm), lambda i: (0, i))]
    out_shape = [jax.ShapeDtypeStruct((T, 3 * NA_W), BF16),
                 jax.ShapeDtypeStruct((qk_w, T), BF16),
                 jax.ShapeDtypeStruct((T, qk_w), BF16),
                 jax.ShapeDtypeStruct((MLA_HEADS, T // MLA_KC, MLA_VT_ROWS, MLA_KC), BF16),
                 jax.ShapeDtypeStruct((GQA_Q_W, T), BF16),
                 jax.ShapeDtypeStruct((T, GQA_KV_W), BF16),
                 jax.ShapeDtypeStruct((GQA_KV_W, T), BF16)]
    return pl.pallas_call(
        _inproj_kernel,
        grid=(T // tm,),
        in_specs=[row(D_MODEL), _const_spec((1, D_MODEL)), _const_spec((D_MODEL, IN1_W)),
                  _const_spec((1, MLA_Q_RANK)), _const_spec((1, MLA_KV_RANK)),
                  _const_spec((qk_w, MLA_Q_RANK)),
                  _const_spec((MLA_KV_RANK, MLA_HEADS * MLA_NOPE)),
                  _const_spec((MLA_HEADS * MLA_V, MLA_KV_RANK)),
                  _const_spec((GQA_Q_W + GQA_KV_W, D_MODEL)),
                  pl.BlockSpec((tm, 6 * LANES), lambda i: (i % tiles_per_seq, 0)),
                  pl.BlockSpec((MLA_ROPE, tm), lambda i: (0, i % tiles_per_seq))],
        out_specs=out_specs,
        out_shape=out_shape,
        compiler_params=_params(1),
        name="inproj",
    )(x2d, g1, w1, g_qa, g_kva, wuqt, wuk, wuvt, wgt, rope, ropet)


NA_ROWS_PER_STEP = 8


def _na_kernel(q_ref, k_ref, v_ref, tb_ref, o_ref, *, rows):
    rb = pl.program_id(1)
    lane = lax.broadcasted_iota(jnp.int32, (GRID_W, LANES), 1)
    left = lane < NA_HEAD_DIM
    n_keys = NA_WIN_R * GRID_W

    def row_body(rr, carry):
        r = rb * NA_ROWS_PER_STEP + rr
        r_start = jnp.clip(r - NA_WIN_R // 2, 0, rows - NA_WIN_R)
        d = r_start - r + (NA_WIN_R - 1)
        tok0 = pl.multiple_of(r_start * GRID_W, GRID_W)
        q0 = pl.multiple_of(rr * GRID_W, GRID_W)
        for hp in range(NA_HEADS // 2):
            cols = slice(hp * LANES, (hp + 1) * LANES)
            q2 = q_ref[pl.ds(q0, GRID_W), cols]
            k2 = k_ref[pl.ds(tok0, n_keys), cols]
            v2 = v_ref[pl.ds(tok0, n_keys), cols]
            outs = []
            for half in range(2):
                hd = 2 * hp + half
                keep = left if half == 0 else jnp.logical_not(left)
                qm = jnp.where(keep, q2, jnp.zeros_like(q2))
                s = _dot_nt(qm, k2)
                bias = jnp.concatenate([tb_ref[hd, d + 2 * j] for j in range(NA_WIN_R // 2)], axis=1)
                s = s + bias
                m = jnp.max(s, axis=-1, keepdims=True)
                p = jnp.exp(s - m)
                l = jnp.sum(p, axis=-1, keepdims=True)
                outs.append(_dot(p.astype(BF16), v2) / l)
            o_ref[pl.ds(q0, GRID_W), cols] = jnp.where(left, outs[0], outs[1]).astype(BF16)
        return carry

    lax.fori_loop(0, NA_ROWS_PER_STEP, row_body, 0)


def _na(na_qkv, tb, *, batch, seq):
    rows = seq // GRID_W
    tq = NA_ROWS_PER_STEP * GRID_W
    qkv3 = na_qkv.reshape(batch, seq, 3 * NA_W)
    full = lambda c: pl.BlockSpec((None, seq, NA_W), lambda b, i: (b, 0, c),
                                  pipeline_mode=pl.Buffered(1))
    return pl.pallas_call(
        functools.partial(_na_kernel, rows=rows),
        grid=(batch, seq // tq),
        in_specs=[pl.BlockSpec((None, tq, NA_W), lambda b, i: (b, i, 0)), full(1), full(2),
                  _const_spec(tb.shape)],
        out_specs=pl.BlockSpec((None, tq, NA_W), lambda b, i: (b, i, 0)),
        out_shape=jax.ShapeDtypeStruct((batch, seq, NA_W), BF16),
        compiler_params=_params(2),
        name="na",
    )(qkv3, qkv3, qkv3, tb)


def _na_bias_table(rpb):
    c = np.arange(GRID_W)
    c_start = np.clip(c - NA_WIN_C // 2, 0, GRID_W - NA_WIN_C)
    kc = np.arange(GRID_W)
    valid = (kc[None, :] >= c_start[:, None]) & (kc[None, :] < c_start[:, None] + NA_WIN_C)
    rel = np.clip(kc[None, :] - c[:, None] + (NA_WIN_C - 1), 0, 2 * NA_WIN_C - 2)
    t2 = jnp.where(valid[None, None], rpb[:, :, rel], NEG_INF)
    return jnp.concatenate([t2[:, :-1], t2[:, 1:]], axis=-1).astype(F32)


def _mla_kernel(qt_ref, k_ref, vt_ref, o_ref, *, n_chunks, log2_scale):
    qt = qt_ref[...]
    tq = qt.shape[1]

    def scores(c):
        k0 = pl.multiple_of(c * MLA_KC, MLA_KC)
        return _dot(k_ref[pl.ds(k0, MLA_KC), :], qt) * log2_scale

    def update(c, m, acc, t):
        m_new = jnp.maximum(m, jnp.max(t, axis=0, keepdims=True))
        alpha = jnp.exp2(m - m_new)
        p = jnp.exp2(t - m_new).astype(BF16)
        return m_new, alpha * acc + _dot(vt_ref[c], p)

    def body(c, carry):
        m, acc, t = carry
        t_next = scores(c + 1)
        m, acc = update(c, m, acc, t)
        return m, acc, t_next

    m0 = jnp.full((1, tq), -jnp.inf, F32)
    acc0 = jnp.zeros((MLA_VT_ROWS, tq), F32)
    m, acc, t = lax.fori_loop(0, n_chunks - 1, body, (m0, acc0, scores(0)), unroll=MLA_UNROLL)
    _, acc = update(n_chunks - 1, m, acc, t)
    o = acc[0:MLA_V, :] / acc[MLA_V:MLA_V + 1, :]
    o_ref[...] = o.T.astype(BF16)


def _mla(mqt, mk, mvt, *, batch, seq, tq):
    n_chunks = seq // MLA_KC
    nq = seq // tq
    k3 = mk.reshape(batch, seq, MLA_HEADS * MLA_QK_PAD)
    return pl.pallas_call(
        functools.partial(_mla_kernel, n_chunks=n_chunks,
                          log2_scale=MLA_QK ** -0.5 * np.log2(np.e)),
        grid=(batch, MLA_HEADS, nq),
        in_specs=[pl.BlockSpec((MLA_QK_PAD, tq), lambda b, h, i: (h, b * nq + i)),
                  pl.BlockSpec((None, seq, MLA_QK_PAD), lambda b, h, i: (b, 0, h)),
                  pl.BlockSpec((None, n_chunks, MLA_VT_ROWS, MLA_KC), lambda b, h, i: (h, b, 0, 0))],
        out_specs=pl.BlockSpec((None, tq, MLA_V), lambda b, h, i: (b, i, h)),
        out_shape=jax.ShapeDtypeStruct((batch, seq, MLA_HEADS * MLA_V), BF16),
        compiler_params=_params(3),
        name="mla",
    )(mqt, k3, mvt)


GQA_BLOCKS_PER_STEP = 4


def _gqa_kernel(qt_ref, kp_ref, kc_ref, kn_ref, vp_ref, vc_ref, vn_ref, b_ref, sk_ref, o_ref, *,
                n_steps):
    i = pl.program_id(1)
    nbs = GQA_BLOCKS_PER_STEP
    k_all = jnp.concatenate([kp_ref[...], kc_ref[...], kn_ref[...]], axis=0)
    vt_all = jnp.concatenate([vp_ref[...], vc_ref[...], vn_ref[...]], axis=1)
    ones_rows = (lax.broadcasted_iota(jnp.int32, (16, 3 * BLOCK), 0) == 0).astype(BF16)
    zeros = jnp.zeros((GQA_HEAD_DIM, GQA_GROUP * BLOCK), BF16)
    for j in range(nbs):
        k2 = k_all[j * BLOCK:(j + 3) * BLOCK, :]
        vt2 = vt_all[:, j * BLOCK:(j + 3) * BLOCK]
        if j == 0:
            bias = b_ref[jnp.where(i == 0, 0, 1)]
        elif j == nbs - 1:
            bias = b_ref[jnp.where(i == n_steps - 1, 2, 1)]
        else:
            bias = b_ref[1]
        bias = jnp.concatenate([bias] * GQA_GROUP, axis=1)
        for kvh in range(GQA_KV_HEADS):
            heads = range(kvh * GQA_GROUP, (kvh + 1) * GQA_GROUP)
            qs = jnp.concatenate([qt_ref[hd * GQA_HEAD_DIM:(hd + 1) * GQA_HEAD_DIM,
                                         j * BLOCK:(j + 1) * BLOCK] for hd in heads], axis=1)
            rhs = jnp.concatenate([qs, zeros] if kvh == 0 else [zeros, qs], axis=0)
            t = _dot(k2, rhs) + bias
            sink = sk_ref[kvh]
            m = jnp.maximum(jnp.max(t, axis=0, keepdims=True), sink)
            p = jnp.exp(t - m).astype(BF16)
            lhs = jnp.concatenate([vt2[kvh * GQA_HEAD_DIM:(kvh + 1) * GQA_HEAD_DIM, :], ones_rows], axis=0)
            acc = _dot(lhs, p)
            l = acc[GQA_HEAD_DIM:GQA_HEAD_DIM + 1, :] + jnp.exp(sink - m)
            o = acc[0:GQA_HEAD_DIM, :] / l
            for g, hd in enumerate(heads):
                o_ref[hd * GQA_HEAD_DIM:(hd + 1) * GQA_HEAD_DIM, j * BLOCK:(j + 1) * BLOCK] = (
                    o[:, g * BLOCK:(g + 1) * BLOCK].astype(BF16))


def _gqa(gqt, gk, gvt, bias3, sink, *, batch, seq):
    nb = seq // BLOCK
    nbs = GQA_BLOCKS_PER_STEP
    n_steps = nb // nbs
    T = batch * seq
    k3 = gk.reshape(batch, seq, GQA_KV_W)
    prev = lambda i: jnp.maximum(i * nbs - 1, 0)
    nxt = lambda i: jnp.minimum((i + 1) * nbs, nb - 1)
    return pl.pallas_call(
        functools.partial(_gqa_kernel, n_steps=n_steps),
        grid=(batch, n_steps),
        in_specs=[pl.BlockSpec((GQA_Q_W, nbs * BLOCK), lambda b, i: (0, b * n_steps + i)),
                  pl.BlockSpec((None, BLOCK, GQA_KV_W), lambda b, i: (b, prev(i), 0)),
                  pl.BlockSpec((None, nbs * BLOCK, GQA_KV_W), lambda b, i: (b, i, 0)),
                  pl.BlockSpec((None, BLOCK, GQA_KV_W), lambda b, i: (b, nxt(i), 0)),
                  pl.BlockSpec((GQA_KV_W, BLOCK), lambda b, i: (0, b * nb + prev(i))),
                  pl.BlockSpec((GQA_KV_W, nbs * BLOCK), lambda b, i: (0, b * n_steps + i)),
                  pl.BlockSpec((GQA_KV_W, BLOCK), lambda b, i: (0, b * nb + nxt(i))),
                  _const_spec((3, 3 * BLOCK, BLOCK)),
                  _const_spec((GQA_KV_HEADS, 1, GQA_GROUP * BLOCK))],
        out_specs=pl.BlockSpec((GQA_Q_W, nbs * BLOCK), lambda b, i: (0, b * n_steps + i)),
        out_shape=jax.ShapeDtypeStruct((GQA_Q_W, T), BF16),
        compiler_params=_params(2),
        name="gqa",
    )(gqt, k3, k3, k3, gvt, gvt, gvt, bias3, sink)


def _gqa_mask_bias(nb):
    i = np.arange(BLOCK)[None, :]
    j = np.arange(3 * BLOCK)[:, None]
    band = np.abs(BLOCK + i - j) <= GQA_WINDOW
    first = band & (j >= BLOCK)
    last = band & (j < 2 * BLOCK)
    if nb == 1:
        first = last = first & last
    m = np.stack([first, band, last])
    return jnp.asarray(np.where(m, 0.0, NEG_INF), F32)


def _mixout_kernel(x_ref, yna_ref, ymla_ref, ygqat_ref, g1_ref, wg_ref, bg_ref, wbr_ref, wo_ref, o_ref):
    x = x_ref[...]
    h = _rms(x, g1_ref[...]).astype(BF16)
    branches = (_dot(yna_ref[...], wbr_ref[0]), _dot(ymla_ref[...], wbr_ref[1]),
                lax.dot_general(ygqat_ref[...], wbr_ref[2], (((0,), (0,)), ((), ())),
                                preferred_element_type=F32))
    merged = None
    for i, br in enumerate(branches):
        cols = slice(i * D_MODEL, (i + 1) * D_MODEL)
        gate = jax.nn.sigmoid(_dot(h, wg_ref[:, cols]) + bg_ref[:, cols])
        merged = gate * br if merged is None else merged + gate * br
    o_ref[...] = x + _dot(merged.astype(BF16), wo_ref[...])


def _mixout(x2d, y_na, y_mla, y_gqa, g1, wg, bg, wbr, wo, *, tm):
    T = x2d.shape[0]
    row = lambda w: pl.BlockSpec((tm, w), lambda i: (i, 0))
    return pl.pallas_call(
        _mixout_kernel,
        grid=(T // tm,),
        in_specs=[row(D_MODEL), row(NA_W), row(MLA_HEADS * MLA_V),
                  pl.BlockSpec((GQA_Q_W, tm), lambda i: (0, i)),
                  _const_spec((1, D_MODEL)), _const_spec((D_MODEL, N_BRANCH * D_MODEL)),
                  _const_spec((1, N_BRANCH * D_MODEL)), _const_spec((N_BRANCH, NA_W, D_MODEL)),
                  _const_spec((D_MODEL, D_MODEL))],
        out_specs=row(D_MODEL),
        out_shape=jax.ShapeDtypeStruct((T, D_MODEL), F32),
        compiler_params=_params(1),
        name="mixout",
    )(x2d, y_na, y_mla, y_gqa, g1, wg, bg, wbr, wo)


def _ffn_kernel(x_ref, xp_ref, xn_ref, g2_ref, wup_ref, cw_ref, cb_ref, wd_ref, fg_ref, o_ref,
                hx_ref, u_ref, acc_ref, *, tm, tiles_per_seq, final):
    j = pl.program_id(0) % tiles_per_seq
    g2 = g2_ref[...]
    x = x_ref[...]
    hp = jnp.where(j == 0, 0.0, _rms(xp_ref[...], g2))
    hn = jnp.where(j == tiles_per_seq - 1, 0.0, _rms(xn_ref[...], g2))
    hx_ref[0:HALO, :] = hp.astype(BF16)
    hx_ref[HALO:HALO + tm, :] = _rms(x, g2).astype(BF16)
    hx_ref[HALO + tm:, :] = hn.astype(BF16)
    acc_ref[...] = jnp.zeros(acc_ref.shape, F32)

    def chunk(c, carry):
        u_ref[...] = _dot(hx_ref[...], wup_ref[c])
        cw = cw_ref[c]
        y = (cw[0:1] * u_ref[pl.ds(HALO - 1, tm), :] + cw[1:2] * u_ref[pl.ds(HALO, tm), :]
             + cw[2:3] * u_ref[pl.ds(HALO + 1, tm), :] + cb_ref[c])
        act = jax.nn.gelu(y[:, FF_CHUNK:]) * y[:, :FF_CHUNK]
        acc_ref[...] += _dot(act.astype(BF16), wd_ref[c])
        return carry

    lax.fori_loop(0, N_FF_CHUNKS, chunk, 0)
    out = x + acc_ref[...]
    if final:
        out = _rms(out, fg_ref[...])
    o_ref[...] = out


def _ffn(x2d, g2, wup, cw, cb, wd, fg, *, seq, tm, final):
    T = x2d.shape[0]
    tiles_per_seq = seq // tm
    hb = tm // HALO
    n_hb = T // HALO
    return pl.pallas_call(
        functools.partial(_ffn_kernel, tm=tm, tiles_per_seq=tiles_per_seq, final=final),
        grid=(T // tm,),
        in_specs=[pl.BlockSpec((tm, D_MODEL), lambda i: (i, 0)),
                  pl.BlockSpec((HALO, D_MODEL), lambda i: (jnp.maximum(i * hb - 1, 0), 0)),
                  pl.BlockSpec((HALO, D_MODEL), lambda i: (jnp.minimum((i + 1) * hb, n_hb - 1), 0)),
                  _const_spec((1, D_MODEL)),
                  _const_spec((N_FF_CHUNKS, D_MODEL, 2 * FF_CHUNK)),
                  _const_spec((N_FF_CHUNKS, 3, 2 * FF_CHUNK)),
                  _const_spec((N_FF_CHUNKS, 1, 2 * FF_CHUNK)),
                  _const_spec((N_FF_CHUNKS, FF_CHUNK, D_MODEL)),
                  _const_spec((1, D_MODEL))],
        out_specs=pl.BlockSpec((tm, D_MODEL), lambda i: (i, 0)),
        out_shape=jax.ShapeDtypeStruct((T, D_MODEL), F32),
        scratch_shapes=[pltpu.VMEM((tm + 2 * HALO, D_MODEL), BF16),
                        pltpu.VMEM((tm + 2 * HALO, 2 * FF_CHUNK), F32),
                        pltpu.VMEM((tm, D_MODEL), F32)],
        compiler_params=_params(1),
        name="ffn",
    )(x2d, x2d, x2d, g2, wup, cw, cb, wd, fg)


def _rope_tables(seq):
    def cs(dim):
        inv = 1.0 / (ROPE_THETA ** (jnp.arange(0, dim, 2, dtype=F32) / dim))
        ang = jnp.arange(seq, dtype=F32)[:, None] * inv[None, :]
        return jnp.cos(ang), jnp.sin(ang)

    def head(c, s):
        z = jnp.zeros_like(s)
        return (jnp.concatenate([c, c], -1), jnp.concatenate([z, s], -1), jnp.concatenate([-s, z], -1))

    m_cos, m_sin = cs(MLA_ROPE)
    mc, ms1, ms2 = head(m_cos, m_sin)
    pad = jnp.zeros_like(mc)
    gc, gs1, gs2 = head(*cs(GQA_HEAD_DIM))
    two = lambda a: jnp.concatenate([a, a], -1)
    table = jnp.concatenate([mc, pad, ms1, pad, ms2, pad, two(gc), two(gs1), two(gs2)], axis=-1)
    table_t = jnp.concatenate([m_cos, m_sin], axis=-1).T
    return table, table_t


def _prep_layer(w_in, b_gate, mla_w_uq, mla_w_ukv, w_br_na, w_br_mla, w_br_gqa, w_out,
                w_up, conv_w, conv_b, w_down):
    o_cq = 3 * NA_W
    o_ckv = o_cq + MLA_Q_RANK
    o_kr = o_ckv + MLA_KV_RANK
    o_gq = o_kr + MLA_ROPE
    o_gkv = o_gq + GQA_Q_W
    o_gate = o_gkv + 2 * GQA_KV_W
    o_gv = o_gkv + GQA_KV_W
    na_scale = NA_HEAD_DIM ** -0.5
    gqa_scale = GQA_HEAD_DIM ** -0.5
    w1 = jnp.concatenate([
        w_in[:, 0:NA_W] * na_scale, w_in[:, NA_W:o_cq],
        w_in[:, o_cq:o_ckv], w_in[:, o_ckv:o_kr],
        w_in[:, o_gkv:o_gv],
        w_in[:, o_kr:o_gq], jnp.zeros((D_MODEL, LANES - MLA_ROPE), w_in.dtype)], axis=1).astype(BF16)
    wgt = jnp.concatenate([w_in[:, o_gq:o_gkv] * gqa_scale, w_in[:, o_gv:o_gate]], axis=1).T.astype(BF16)
    wg = w_in[:, o_gate:].astype(BF16)
    wuqt = jnp.pad(mla_w_uq.reshape(MLA_Q_RANK, MLA_HEADS, MLA_QK),
                   ((0, 0), (0, 0), (0, MLA_QK_PAD - MLA_QK))).reshape(MLA_Q_RANK, -1).T.astype(BF16)
    ukv = mla_w_ukv.reshape(MLA_KV_RANK, MLA_HEADS, MLA_NOPE + MLA_V)
    wuk = ukv[:, :, :MLA_NOPE].reshape(MLA_KV_RANK, -1).astype(BF16)
    wuvt = ukv[:, :, MLA_NOPE:].reshape(MLA_KV_RANK, -1).T.astype(BF16)
    wbr = jnp.stack([w_br_na, w_br_mla, w_br_gqa]).astype(BF16)
    chunks = lambda a: a.reshape(a.shape[0], 2, N_FF_CHUNKS, FF_CHUNK)
    wup = chunks(w_up).transpose(2, 0, 1, 3).reshape(N_FF_CHUNKS, D_MODEL, 2 * FF_CHUNK).astype(BF16)
    cw = chunks(conv_w).transpose(2, 0, 1, 3).reshape(N_FF_CHUNKS, 3, 2 * FF_CHUNK)
    cb = chunks(conv_b[None]).transpose(2, 0, 1, 3).reshape(N_FF_CHUNKS, 1, 2 * FF_CHUNK)
    wd = w_down.reshape(N_FF_CHUNKS, FF_CHUNK, D_MODEL).astype(BF16)
    return dict(w1=w1, wgt=wgt, wg=wg, bg=b_gate[None], wuqt=wuqt, wuk=wuk, wuvt=wuvt, wbr=wbr,
                wo=w_out.astype(BF16), wup=wup, cw=cw, cb=cb, wd=wd)


def _tiles(seq):
    tm = min(512, seq)
    tq = min(1024, seq)
    return tm, tq


def kernel(x, norm1_g, w_in, b_gate, na_rpb, mla_qa_g, mla_kva_g, mla_w_uq, mla_w_ukv, gqa_sink, w_br_na, w_br_mla, w_br_gqa, w_out, norm2_g, w_up, conv_w, conv_b, w_down, final_g):
    batch, seq, _ = x.shape
    depth = w_in.shape[0]
    assert seq % (NA_ROWS_PER_STEP * GRID_W) == 0 and seq // GRID_W >= NA_WIN_R
    assert seq % (GQA_BLOCKS_PER_STEP * BLOCK) == 0 and MLA_ROPE == GQA_HEAD_DIM
    tm, tq = _tiles(seq)
    rope, rope_t = _rope_tables(seq)
    gqa_bias = _gqa_mask_bias(seq // BLOCK)
    xf = x.reshape(batch * seq, D_MODEL)
    for l in range(depth):
        p = _prep_layer(w_in[l], b_gate[l], mla_w_uq[l], mla_w_ukv[l], w_br_na[l], w_br_mla[l],
                        w_br_gqa[l], w_out[l], w_up[l], conv_w[l], conv_b[l], w_down[l])
        g1 = norm1_g[l][None]
        na_qkv, mqt, mk, mvt, gqt, gk, gvt = _inproj(
            xf, g1, p["w1"], mla_qa_g[l][None], mla_kva_g[l][None], p["wuqt"], p["wuk"], p["wuvt"],
            p["wgt"], rope, rope_t, seq=seq, tm=tm)
        y_na = _na(na_qkv, _na_bias_table(na_rpb[l]), batch=batch, seq=seq)
        y_mla = _mla(mqt, mk, mvt, batch=batch, seq=seq, tq=tq)
        sink = jnp.repeat(gqa_sink[l].astype(F32).reshape(GQA_KV_HEADS, 1, GQA_GROUP), BLOCK, axis=-1)
        y_gqa_t = _gqa(gqt, gk, gvt, gqa_bias, sink, batch=batch, seq=seq)
        flat = lambda y: y.reshape(batch * seq, -1)
        xf = _mixout(xf, flat(y_na), flat(y_mla), y_gqa_t, g1, p["wg"], p["bg"], p["wbr"], p["wo"],
                     tm=tm)
        xf = _ffn(xf, norm2_g[l][None], p["wup"], p["cw"], p["cb"], p["wd"], final_g[None],
                  seq=seq, tm=tm, final=(l == depth - 1))
    return xf.reshape(batch, seq, D_MODEL)
```

```python
import functools

import jax
import jax.numpy as jnp
import numpy as np
from jax import lax
from jax.experimental import pallas as pl
from jax.experimental.pallas import tpu as pltpu

F32 = jnp.float32
BF16 = jnp.bfloat16

D_MODEL = 1024
GRID_W = 64
NA_HEADS = 8
NA_HEAD_DIM = 64
NA_WIN_R = 8
NA_WIN_C = 16
NA_W = NA_HEADS * NA_HEAD_DIM
MLA_HEADS = 4
MLA_Q_RANK = 384
MLA_KV_RANK = 256
MLA_NOPE = 128
MLA_ROPE = 64
MLA_V = 128
MLA_QK = MLA_NOPE + MLA_ROPE
MLA_QK_PAD = 256
MLA_KC = 512
MLA_VT_ROWS = MLA_V + 16
MLA_LOG2_SCALE = float(MLA_QK ** -0.5 * np.log2(np.e))
GQA_HEADS = 8
GQA_KV_HEADS = 2
GQA_GROUP = GQA_HEADS // GQA_KV_HEADS
GQA_HEAD_DIM = 64
GQA_WINDOW = 128
GQA_Q_W = GQA_HEADS * GQA_HEAD_DIM
GQA_KV_W = GQA_KV_HEADS * GQA_HEAD_DIM
BLOCK = 128
N_BRANCH = 3
D_FF = 2816
ROPE_THETA = 10000.0
EPS = 1e-6
NEG_INF = -1e30

LANES = 128
ONES_ROWS = 16
IN1_W = 1408
IN2_W = 1664
FF_CHUNK = 256
N_FF_CHUNKS = D_FF // FF_CHUNK
FF_UNROLL = True
HALO = 16
VMEM_LIMIT = 56 * 1024 * 1024

_O_NAK = 0
_O_CQ = 512
_O_CKV = 896
_O_GK = 1152
_O_KR = 1280
_T_NAQ = 0
_T_NAV = 512
_T_GQ = 1024
_T_GV = 1536


def _rms(x, g):
    return x * lax.rsqrt(jnp.mean(x * x, axis=-1, keepdims=True) + EPS) * g


def _rope128(blk, c, s1, s2):
    return blk * c + pltpu.roll(blk, 32, 1) * s1 + pltpu.roll(blk, 96, 1) * s2


def _dot(a, b):
    return jnp.dot(a, b, preferred_element_type=F32)


def _dot_nt(a, b):
    return lax.dot_general(a, b, (((1,), (1,)), ((), ())), preferred_element_type=F32)


def _dot_tn(a, b):
    return lax.dot_general(a, b, (((0,), (0,)), ((), ())), preferred_element_type=F32)


def _ones_rows(n):
    return (lax.broadcasted_iota(jnp.int32, (ONES_ROWS, n), 0) == 0).astype(BF16)


def _params(n_axes):
    return pltpu.CompilerParams(dimension_semantics=("arbitrary",) * n_axes,
                                vmem_limit_bytes=VMEM_LIMIT)


def _const_spec(shape):
    n = len(shape)
    return pl.BlockSpec(shape, lambda *_: (0,) * n, pipeline_mode=pl.Buffered(1))


def _inproj_kernel(x_ref, g1_ref, w1_ref, gqa_ref, gkva_ref, wuqt_ref, wuk_ref, wuvt_ref, wgt_ref,
                   rope_ref, ropet_ref, naqt_ref, nak_ref, navt_ref, mqt_ref, mk_ref, mvt_ref,
                   gqt_ref, gk_ref, gvt_ref):
    tm = x_ref.shape[0]
    h = _rms(x_ref[...], g1_ref[...]).astype(BF16)
    z = _dot(h, w1_ref[...])
    gt = _dot_nt(wgt_ref[...], h)

    naqt_ref[...] = gt[_T_NAQ:_T_NAV, :].astype(BF16)
    for c in range(tm // LANES):
        navt_ref[c] = gt[_T_NAV:_T_GQ, c * LANES:(c + 1) * LANES].astype(BF16)
    nak_ref[...] = z[:, _O_NAK:_O_CQ].astype(BF16)

    mc, ms1, ms2 = rope_ref[:, 0:128], rope_ref[:, 128:256], rope_ref[:, 256:384]
    gc, gs1, gs2 = rope_ref[:, 384:512], rope_ref[:, 512:640], rope_ref[:, 640:768]

    cqn = _rms(z[:, _O_CQ:_O_CKV], gqa_ref[...]).astype(BF16)
    qt = _dot_nt(wuqt_ref[...], cqn) * MLA_LOG2_SCALE
    half = MLA_ROPE // 2
    ct, st = ropet_ref[0:half, :], ropet_ref[half:MLA_ROPE, :]
    for hd in range(MLA_HEADS):
        o = hd * MLA_QK_PAD
        r = o + MLA_NOPE
        mqt_ref[o:r, :] = qt[o:r, :].astype(BF16)
        x1, x2 = qt[r:r + half, :], qt[r + half:r + MLA_ROPE, :]
        mqt_ref[r:r + half, :] = (x1 * ct - x2 * st).astype(BF16)
        mqt_ref[r + half:r + MLA_ROPE, :] = (x1 * st + x2 * ct).astype(BF16)
        mqt_ref[r + MLA_ROPE:o + MLA_QK_PAD, :] = qt[r + MLA_ROPE:o + MLA_QK_PAD, :].astype(BF16)

    ckvn = _rms(z[:, _O_CKV:_O_GK], gkva_ref[...]).astype(BF16)
    kn = _dot(ckvn, wuk_ref[...])
    kpe = _rope128(z[:, _O_KR:_O_KR + LANES], mc, ms1, ms2).astype(BF16)
    for hd in range(MLA_HEADS):
        o = hd * MLA_QK_PAD
        mk_ref[:, o:o + LANES] = kn[:, hd * LANES:(hd + 1) * LANES].astype(BF16)
        mk_ref[:, o + LANES:o + 2 * LANES] = kpe
    vt = _dot_nt(wuvt_ref[...], ckvn)
    ones_rows = _ones_rows(MLA_KC)
    for hd in range(MLA_HEADS):
        for c in range(tm // MLA_KC):
            mvt_ref[hd, c, 0:MLA_V, :] = vt[hd * MLA_V:(hd + 1) * MLA_V,
                                            c * MLA_KC:(c + 1) * MLA_KC].astype(BF16)
            mvt_ref[hd, c, MLA_V:MLA_VT_ROWS, :] = ones_rows

    for hd in range(GQA_HEADS):
        r = _T_GQ + hd * GQA_HEAD_DIM
        x1, x2 = gt[r:r + half, :], gt[r + half:r + GQA_HEAD_DIM, :]
        o = hd * GQA_HEAD_DIM
        gqt_ref[o:o + half, :] = (x1 * ct - x2 * st).astype(BF16)
        gqt_ref[o + half:o + GQA_HEAD_DIM, :] = (x1 * st + x2 * ct).astype(BF16)
    gvt_ref[...] = gt[_T_GV:, :].astype(BF16)
    gk_ref[...] = _rope128(z[:, _O_GK:_O_KR], gc, gs1, gs2).astype(BF16)


def _inproj(x2d, g1, w1, g_qa, g_kva, wuqt, wuk, wuvt, wgt, rope, ropet, *, seq, tm):
    T = x2d.shape[0]
    tiles_per_seq = seq // tm
    cpt = tm // MLA_KC
    row = lambda w: pl.BlockSpec((tm, w), lambda i: (i, 0))
    col = lambda w: pl.BlockSpec((w, tm), lambda i: (0, i))
    qk_w = MLA_HEADS * MLA_QK_PAD
    out_specs = [col(NA_W), row(NA_W),
                 pl.BlockSpec((tm // LANES, NA_W, LANES), lambda i: (i, 0, 0)),
                 col(qk_w), row(qk_w),
                 pl.BlockSpec((MLA_HEADS, cpt, MLA_VT_ROWS, MLA_KC), lambda i: (0, i, 0, 0)),
                 col(GQA_Q_W), row(GQA_KV_W), col(GQA_KV_W)]
    out_shape = [jax.ShapeDtypeStruct((NA_W, T), BF16),
                 jax.ShapeDtypeStruct((T, NA_W), BF16),
                 jax.ShapeDtypeStruct((T // LANES, NA_W, LANES), BF16),
                 jax.ShapeDtypeStruct((qk_w, T), BF16),
                 jax.ShapeDtypeStruct((T, qk_w), BF16),
                 jax.ShapeDtypeStruct((MLA_HEADS, T // MLA_KC, MLA_VT_ROWS, MLA_KC), BF16),
                 jax.ShapeDtypeStruct((GQA_Q_W, T), BF16),
                 jax.ShapeDtypeStruct((T, GQA_KV_W), BF16),
                 jax.ShapeDtypeStruct((GQA_KV_W, T), BF16)]
    return pl.pallas_call(
        _inproj_kernel,
        grid=(T // tm,),
        in_specs=[row(D_MODEL), _const_spec((1, D_MODEL)), _const_spec((D_MODEL, IN1_W)),
                  _const_spec((1, MLA_Q_RANK)), _const_spec((1, MLA_KV_RANK)),
                  _const_spec((qk_w, MLA_Q_RANK)),
                  _const_spec((MLA_KV_RANK, MLA_HEADS * MLA_NOPE)),
                  _const_spec((MLA_HEADS * MLA_V, MLA_KV_RANK)),
                  _const_spec((IN2_W, D_MODEL)),
                  pl.BlockSpec((tm, 6 * LANES), lambda i: (i % tiles_per_seq, 0)),
                  pl.BlockSpec((MLA_ROPE, tm), lambda i: (0, i % tiles_per_seq))],
        out_specs=out_specs,
        out_shape=out_shape,
        compiler_params=_params(1),
        name="inproj",
    )(x2d, g1, w1, g_qa, g_kva, wuqt, wuk, wuvt, wgt, rope, ropet)


NA_ROWS_PER_STEP = 8
NA_KEY_ROWS = 10
NA_CASES = 5


def _na_kernel(qt_ref, k_ref, vt_ref, tb_ref, o_ref, *, rows):
    step = pl.program_id(1)
    n_keys = NA_KEY_ROWS * GRID_W
    ones_rows = _ones_rows(n_keys)
    zeros = jnp.zeros((NA_HEAD_DIM, LANES), BF16)
    pairs = []
    for rp in range(NA_ROWS_PER_STEP // 2):
        r = step * NA_ROWS_PER_STEP + 2 * rp
        case = jnp.where(r == 0, 0, jnp.where(r == 2, 1, jnp.where(
            r == rows - 4, 3, jnp.where(r == rows - 2, 4, 2))))
        u = jnp.clip(r - NA_WIN_R // 2, 0, rows - NA_KEY_ROWS)
        pairs.append((case, pl.multiple_of(u * GRID_W, LANES), u // 2))

    def scores(rp, hp):
        case, tok0, _ = pairs[rp]
        lo, mid, hi = hp * LANES, hp * LANES + NA_HEAD_DIM, (hp + 1) * LANES
        qcols = slice(rp * LANES, (rp + 1) * LANES)
        qa, qb = qt_ref[lo:mid, qcols], qt_ref[mid:hi, qcols]
        rhs = jnp.concatenate([jnp.concatenate([qa, zeros], axis=1),
                               jnp.concatenate([zeros, qb], axis=1)], axis=0)
        bias = jnp.concatenate([tb_ref[case, 2 * hp], tb_ref[case, 2 * hp + 1]], axis=1)
        return _dot(k_ref[pl.ds(tok0, n_keys), lo:hi], rhs) + bias

    def finish(rp, hp, t):
        _, _, c0 = pairs[rp]
        lo, mid, hi = hp * LANES, hp * LANES + NA_HEAD_DIM, (hp + 1) * LANES
        qcols = slice(rp * LANES, (rp + 1) * LANES)
        m = jnp.max(t, axis=0, keepdims=True)
        p = jnp.exp(t - m).astype(BF16)
        vt_win = jnp.concatenate([vt_ref[c0 + j, lo:hi, :] for j in range(n_keys // LANES)], axis=1)
        lhs = jnp.concatenate([vt_win, ones_rows], axis=0)
        acc = _dot(lhs, p)
        l = acc[LANES:LANES + 1, :]
        o_ref[lo:mid, qcols] = (acc[0:NA_HEAD_DIM, 0:LANES] / l[:, 0:LANES]).astype(BF16)
        o_ref[mid:hi, qcols] = (acc[NA_HEAD_DIM:LANES, LANES:] / l[:, LANES:]).astype(BF16)

    units = [(rp, hp) for rp in range(NA_ROWS_PER_STEP // 2) for hp in range(NA_HEADS // 2)]
    t = scores(*units[0])
    for idx, unit in enumerate(units):
        t_next = scores(*units[idx + 1]) if idx + 1 < len(units) else None
        finish(*unit, t)
        t = t_next


def _na(naqt, nak, navt, tb, *, batch, seq):
    rows = seq // GRID_W
    tq = NA_ROWS_PER_STEP * GRID_W
    n_steps = seq // tq
    T = batch * seq
    k3 = nak.reshape(batch, seq, NA_W)
    vt4 = navt.reshape(batch, seq // LANES, NA_W, LANES)
    return pl.pallas_call(
        functools.partial(_na_kernel, rows=rows),
        grid=(batch, n_steps),
        in_specs=[pl.BlockSpec((NA_W, tq), lambda b, i: (0, b * n_steps + i)),
                  pl.BlockSpec((None, seq, NA_W), lambda b, i: (b, 0, 0), pipeline_mode=pl.Buffered(1)),
                  pl.BlockSpec((None, seq // LANES, NA_W, LANES), lambda b, i: (b, 0, 0, 0),
                               pipeline_mode=pl.Buffered(1)),
                  _const_spec(tb.shape)],
        out_specs=pl.BlockSpec((NA_W, tq), lambda b, i: (0, b * n_steps + i)),
        out_shape=jax.ShapeDtypeStruct((NA_W, T), BF16),
        compiler_params=_params(2),
        name="na",
    )(naqt, k3, vt4, tb)


def _na_bias_table(rpb, rows):
    i = np.arange(NA_KEY_ROWS)[:, None, None, None]
    kc = np.arange(GRID_W)[None, :, None, None]
    rr = np.arange(2)[None, None, :, None]
    c = np.arange(GRID_W)[None, None, None, :]
    c_start = np.clip(c - NA_WIN_C // 2, 0, GRID_W - NA_WIN_C)
    valid_c = (kc >= c_start) & (kc < c_start + NA_WIN_C)
    cidx = np.clip(kc - c + (NA_WIN_C - 1), 0, 2 * NA_WIN_C - 2)
    full = (NA_KEY_ROWS, GRID_W, 2, GRID_W)
    flat = lambda a: np.broadcast_to(a, full).reshape(NA_KEY_ROWS * GRID_W, 2 * GRID_W)
    tabs = []
    for r in (0, 2, 4, rows - 4, rows - 2):
        u = np.clip(r - NA_WIN_R // 2, 0, rows - NA_KEY_ROWS)
        r_start = np.clip(r + rr - NA_WIN_R // 2, 0, rows - NA_WIN_R)
        key_row = u + i
        valid = (key_row >= r_start) & (key_row < r_start + NA_WIN_R) & valid_c
        ridx = np.clip(key_row - (r + rr) + (NA_WIN_R - 1), 0, 2 * NA_WIN_R - 2)
        tabs.append(jnp.where(flat(valid)[None], rpb[:, flat(ridx), flat(cidx)], NEG_INF))
    return jnp.stack(tabs).astype(F32)


def _mla_kernel(qt_ref, k_ref, vt_ref, o_ref, *, n_chunks):
    qt = qt_ref[...]
    tq = qt.shape[1]

    def scores(c):
        k0 = pl.multiple_of(c * MLA_KC, MLA_KC)
        return _dot(k_ref[pl.ds(k0, MLA_KC), :], qt)

    def update(c, m, acc, t):
        m_new = jnp.maximum(m, jnp.max(t, axis=0, keepdims=True))
        alpha = jnp.exp2(m - m_new)
        p = jnp.exp2(t - m_new).astype(BF16)
        return m_new, alpha * acc + _dot(vt_ref[c], p)

    def body(c, carry):
        m, acc, t = carry
        t_next = scores(c + 1)
        m, acc = update(c, m, acc, t)
        return m, acc, t_next

    m0 = jnp.full((1, tq), -jnp.inf, F32)
    acc0 = jnp.zeros((MLA_VT_ROWS, tq), F32)
    m, acc, t = lax.fori_loop(0, n_chunks - 1, body, (m0, acc0, scores(0)), unroll=True)
    _, acc = update(n_chunks - 1, m, acc, t)
    o = acc[0:MLA_V, :] / acc[MLA_V:MLA_V + 1, :]
    o_ref[...] = o.T.astype(BF16)


def _mla(mqt, mk, mvt, *, batch, seq, tq):
    n_chunks = seq // MLA_KC
    nq = seq // tq
    k3 = mk.reshape(batch, seq, MLA_HEADS * MLA_QK_PAD)
    return pl.pallas_call(
        functools.partial(_mla_kernel, n_chunks=n_chunks),
        grid=(batch, MLA_HEADS, nq),
        in_specs=[pl.BlockSpec((MLA_QK_PAD, tq), lambda b, h, i: (h, b * nq + i)),
                  pl.BlockSpec((None, seq, MLA_QK_PAD), lambda b, h, i: (b, 0, h)),
                  pl.BlockSpec((None, n_chunks, MLA_VT_ROWS, MLA_KC), lambda b, h, i: (h, b, 0, 0))],
        out_specs=pl.BlockSpec((None, tq, MLA_V), lambda b, h, i: (b, i, h)),
        out_shape=jax.ShapeDtypeStruct((batch, seq, MLA_HEADS * MLA_V), BF16),
        compiler_params=_params(3),
        name="mla",
    )(mqt, k3, mvt)


GQA_BLOCKS_PER_STEP = 4


def _gqa_kernel(qt_ref, kp_ref, kc_ref, kn_ref, vp_ref, vc_ref, vn_ref, b_ref, sk_ref, o_ref, *,
                n_steps):
    i = pl.program_id(1)
    nbs = GQA_BLOCKS_PER_STEP
    k_all = jnp.concatenate([kp_ref[...], kc_ref[...], kn_ref[...]], axis=0)
    vt_all = jnp.concatenate([vp_ref[...], vc_ref[...], vn_ref[...]], axis=1)
    ones_rows = _ones_rows(3 * BLOCK)
    zeros = jnp.zeros((GQA_HEAD_DIM, GQA_GROUP * BLOCK), BF16)

    def scores(j, kvh):
        if j == 0:
            bias = b_ref[jnp.where(i == 0, 0, 1)]
        elif j == nbs - 1:
            bias = b_ref[jnp.where(i == n_steps - 1, 2, 1)]
        else:
            bias = b_ref[1]
        bias = jnp.concatenate([bias] * GQA_GROUP, axis=1)
        heads = range(kvh * GQA_GROUP, (kvh + 1) * GQA_GROUP)
        qs = jnp.concatenate([qt_ref[hd * GQA_HEAD_DIM:(hd + 1) * GQA_HEAD_DIM,
                                     j * BLOCK:(j + 1) * BLOCK] for hd in heads], axis=1)
        rhs = jnp.concatenate([qs, zeros] if kvh == 0 else [zeros, qs], axis=0)
        return _dot(k_all[j * BLOCK:(j + 3) * BLOCK, :], rhs) + bias

    def finish(j, kvh, t):
        sink = sk_ref[kvh]
        m = jnp.maximum(jnp.max(t, axis=0, keepdims=True), sink)
        p = jnp.exp(t - m).astype(BF16)
        vt2 = vt_all[kvh * GQA_HEAD_DIM:(kvh + 1) * GQA_HEAD_DIM, j * BLOCK:(j + 3) * BLOCK]
        acc = _dot(jnp.concatenate([vt2, ones_rows], axis=0), p)
        l = acc[GQA_HEAD_DIM:GQA_HEAD_DIM + 1, :] + jnp.exp(sink - m)
        o = acc[0:GQA_HEAD_DIM, :] / l
        for g in range(GQA_GROUP):
            hd = kvh * GQA_GROUP + g
            o_ref[hd * GQA_HEAD_DIM:(hd + 1) * GQA_HEAD_DIM, j * BLOCK:(j + 1) * BLOCK] = (
                o[:, g * BLOCK:(g + 1) * BLOCK].astype(BF16))

    units = [(j, kvh) for j in range(nbs) for kvh in range(GQA_KV_HEADS)]
    t = scores(*units[0])
    for idx, unit in enumerate(units):
        t_next = scores(*units[idx + 1]) if idx + 1 < len(units) else None
        finish(*unit, t)
        t = t_next


def _gqa(gqt, gk, gvt, bias3, sink, *, batch, seq):
    nb = seq // BLOCK
    nbs = GQA_BLOCKS_PER_STEP
    n_steps = nb // nbs
    T = batch * seq
    k3 = gk.reshape(batch, seq, GQA_KV_W)
    prev = lambda i: jnp.maximum(i * nbs - 1, 0)
    nxt = lambda i: jnp.minimum((i + 1) * nbs, nb - 1)
    return pl.pallas_call(
        functools.partial(_gqa_kernel, n_steps=n_steps),
        grid=(batch, n_steps),
        in_specs=[pl.BlockSpec((GQA_Q_W, nbs * BLOCK), lambda b, i: (0, b * n_steps + i)),
                  pl.BlockSpec((None, BLOCK, GQA_KV_W), lambda b, i: (b, prev(i), 0)),
                  pl.BlockSpec((None, nbs * BLOCK, GQA_KV_W), lambda b, i: (b, i, 0)),
                  pl.BlockSpec((None, BLOCK, GQA_KV_W), lambda b, i: (b, nxt(i), 0)),
                  pl.BlockSpec((GQA_KV_W, BLOCK), lambda b, i: (0, b * nb + prev(i))),
                  pl.BlockSpec((GQA_KV_W, nbs * BLOCK), lambda b, i: (0, b * n_steps + i)),
                  pl.BlockSpec((GQA_KV_W, BLOCK), lambda b, i: (0, b * nb + nxt(i))),
                  _const_spec((3, 3 * BLOCK, BLOCK)),
                  _const_spec((GQA_KV_HEADS, 1, GQA_GROUP * BLOCK))],
        out_specs=pl.BlockSpec((GQA_Q_W, nbs * BLOCK), lambda b, i: (0, b * n_steps + i)),
        out_shape=jax.ShapeDtypeStruct((GQA_Q_W, T), BF16),
        compiler_params=_params(2),
        name="gqa",
    )(gqt, k3, k3, k3, gvt, gvt, gvt, bias3, sink)


def _gqa_mask_bias(nb):
    i = np.arange(BLOCK)[None, :]
    j = np.arange(3 * BLOCK)[:, None]
    band = np.abs(BLOCK + i - j) <= GQA_WINDOW
    first = band & (j >= BLOCK)
    last = band & (j < 2 * BLOCK)
    if nb == 1:
        first = last = first & last
    m = np.stack([first, band, last])
    return jnp.asarray(np.where(m, 0.0, NEG_INF), F32)


def _mixout_kernel(x_ref, ynat_ref, ymla_ref, ygqat_ref, g1_ref, wg_ref, bg_ref, wbr_ref, wo_ref, o_ref):
    x = x_ref[...]
    h = _rms(x, g1_ref[...]).astype(BF16)
    branches = (_dot_tn(ynat_ref[...], wbr_ref[0]), _dot(ymla_ref[...], wbr_ref[1]),
                _dot_tn(ygqat_ref[...], wbr_ref[2]))
    merged = None
    for i, br in enumerate(branches):
        cols = slice(i * D_MODEL, (i + 1) * D_MODEL)
        gate = jax.nn.sigmoid(_dot(h, wg_ref[:, cols]) + bg_ref[:, cols])
        merged = gate * br if merged is None else merged + gate * br
    o_ref[...] = x + _dot(merged.astype(BF16), wo_ref[...])


def _mixout(x2d, y_na_t, y_mla, y_gqa_t, g1, wg, bg, wbr, wo, *, tm):
    T = x2d.shape[0]
    row = lambda w: pl.BlockSpec((tm, w), lambda i: (i, 0))
    col = lambda w: pl.BlockSpec((w, tm), lambda i: (0, i))
    return pl.pallas_call(
        _mixout_kernel,
        grid=(T // tm,),
        in_specs=[row(D_MODEL), col(NA_W), row(MLA_HEADS * MLA_V), col(GQA_Q_W),
                  _const_spec((1, D_MODEL)), _const_spec((D_MODEL, N_BRANCH * D_MODEL)),
                  _const_spec((1, N_BRANCH * D_MODEL)), _const_spec((N_BRANCH, NA_W, D_MODEL)),
                  _const_spec((D_MODEL, D_MODEL))],
        out_specs=row(D_MODEL),
        out_shape=jax.ShapeDtypeStruct((T, D_MODEL), F32),
        compiler_params=_params(1),
        name="mixout",
    )(x2d, y_na_t, y_mla, y_gqa_t, g1, wg, bg, wbr, wo)


def _ffn_kernel(x_ref, xp_ref, xn_ref, g2_ref, wup_ref, cw_ref, cb_ref, wd_ref, fg_ref, o_ref,
                hx_ref, u_ref, acc_ref, *, tm, tiles_per_seq, final):
    j = pl.program_id(0) % tiles_per_seq
    g2 = g2_ref[...]
    x = x_ref[...]
    hp = jnp.where(j == 0, 0.0, _rms(xp_ref[...], g2))
    hn = jnp.where(j == tiles_per_seq - 1, 0.0, _rms(xn_ref[...], g2))
    hx_ref[0:HALO, :] = hp.astype(BF16)
    hx_ref[HALO:HALO + tm, :] = _rms(x, g2).astype(BF16)
    hx_ref[HALO + tm:, :] = hn.astype(BF16)

    def up(c):
        u_ref[c % 2] = _dot(hx_ref[...], wup_ref[c])

    def down(c):
        ub = u_ref.at[c % 2]
        cw = cw_ref[c]
        y = (cw[0:1] * ub[pl.ds(HALO - 1, tm), :] + cw[1:2] * ub[pl.ds(HALO, tm), :]
             + cw[2:3] * ub[pl.ds(HALO + 1, tm), :] + cb_ref[c])
        act = jax.nn.gelu(y[:, FF_CHUNK:]) * y[:, :FF_CHUNK]
        acc_ref[...] += _dot(act.astype(BF16), wd_ref[c])

    acc_ref[...] = jnp.zeros(acc_ref.shape, F32)
    up(0)

    def body(c, carry):
        up(c + 1)
        down(c)
        return carry

    lax.fori_loop(0, N_FF_CHUNKS - 1, body, 0, unroll=FF_UNROLL)
    down(N_FF_CHUNKS - 1)
    out = x + acc_ref[...]
    if final:
        out = _rms(out, fg_ref[...])
    o_ref[...] = out


def _ffn(x2d, g2, wup, cw, cb, wd, fg, *, seq, tm, final):
    T = x2d.shape[0]
    tiles_per_seq = seq // tm
    hb = tm // HALO
    n_hb = T // HALO
    return pl.pallas_call(
        functools.partial(_ffn_kernel, tm=tm, tiles_per_seq=tiles_per_seq, final=final),
        grid=(T // tm,),
        in_specs=[pl.BlockSpec((tm, D_MODEL), lambda i: (i, 0)),
                  pl.BlockSpec((HALO, D_MODEL), lambda i: (jnp.maximum(i * hb - 1, 0), 0)),
                  pl.BlockSpec((HALO, D_MODEL), lambda i: (jnp.minimum((i + 1) * hb, n_hb - 1), 0)),
                  _const_spec((1, D_MODEL)),
                  _const_spec((N_FF_CHUNKS, D_MODEL, 2 * FF_CHUNK)),
                  _const_spec((N_FF_CHUNKS, 3, 2 * FF_CHUNK)),
                  _const_spec((N_FF_CHUNKS, 1, 2 * FF_CHUNK)),
                  _const_spec((N_FF_CHUNKS, FF_CHUNK, D_MODEL)),
                  _const_spec((1, D_MODEL))],
        out_specs=pl.BlockSpec((tm, D_MODEL), lambda i: (i, 0)),
        out_shape=jax.ShapeDtypeStruct((T, D_MODEL), F32),
        scratch_shapes=[pltpu.VMEM((tm + 2 * HALO, D_MODEL), BF16),
                        pltpu.VMEM((2, tm + 2 * HALO, 2 * FF_CHUNK), F32),
                        pltpu.VMEM((tm, D_MODEL), F32)],
        compiler_params=_params(1),
        name="ffn",
    )(x2d, x2d, x2d, g2, wup, cw, cb, wd, fg)


def _rope_tables(seq):
    def cs(dim):
        inv = 1.0 / (ROPE_THETA ** (jnp.arange(0, dim, 2, dtype=F32) / dim))
        ang = jnp.arange(seq, dtype=F32)[:, None] * inv[None, :]
        return jnp.cos(ang), jnp.sin(ang)

    def head(c, s):
        z = jnp.zeros_like(s)
        return (jnp.concatenate([c, c], -1), jnp.concatenate([z, s], -1), jnp.concatenate([-s, z], -1))

    m_cos, m_sin = cs(MLA_ROPE)
    mc, ms1, ms2 = head(m_cos, m_sin)
    pad = jnp.zeros_like(mc)
    gc, gs1, gs2 = head(*cs(GQA_HEAD_DIM))
    two = lambda a: jnp.concatenate([a, a], -1)
    table = jnp.concatenate([mc, pad, ms1, pad, ms2, pad, two(gc), two(gs1), two(gs2)], axis=-1)
    table_t = jnp.concatenate([m_cos, m_sin], axis=-1).T
    return table, table_t


def _prep_layer(w_in, b_gate, mla_w_uq, mla_w_ukv, w_br_na, w_br_mla, w_br_gqa, w_out,
                w_up, conv_w, conv_b, w_down):
    o_nak = NA_W
    o_nav = 2 * NA_W
    o_cq = 3 * NA_W
    o_ckv = o_cq + MLA_Q_RANK
    o_kr = o_ckv + MLA_KV_RANK
    o_gq = o_kr + MLA_ROPE
    o_gkv = o_gq + GQA_Q_W
    o_gv = o_gkv + GQA_KV_W
    o_gate = o_gv + GQA_KV_W
    na_scale = NA_HEAD_DIM ** -0.5
    gqa_scale = GQA_HEAD_DIM ** -0.5
    w1 = jnp.concatenate([
        w_in[:, o_nak:o_nav], w_in[:, o_cq:o_ckv], w_in[:, o_ckv:o_kr], w_in[:, o_gkv:o_gv],
        w_in[:, o_kr:o_gq], jnp.zeros((D_MODEL, LANES - MLA_ROPE), w_in.dtype)], axis=1).astype(BF16)
    wgt = jnp.concatenate([w_in[:, 0:o_nak] * na_scale, w_in[:, o_nav:o_cq],
                           w_in[:, o_gq:o_gkv] * gqa_scale, w_in[:, o_gv:o_gate]], axis=1).T.astype(BF16)
    wg = w_in[:, o_gate:].astype(BF16)
    wuqt = jnp.pad(mla_w_uq.reshape(MLA_Q_RANK, MLA_HEADS, MLA_QK),
                   ((0, 0), (0, 0), (0, MLA_QK_PAD - MLA_QK))).reshape(MLA_Q_RANK, -1).T.astype(BF16)
    ukv = mla_w_ukv.reshape(MLA_KV_RANK, MLA_HEADS, MLA_NOPE + MLA_V)
    wuk = ukv[:, :, :MLA_NOPE].reshape(MLA_KV_RANK, -1).astype(BF16)
    wuvt = ukv[:, :, MLA_NOPE:].reshape(MLA_KV_RANK, -1).T.astype(BF16)
    wbr = jnp.stack([w_br_na, w_br_mla, w_br_gqa]).astype(BF16)
    chunks = lambda a: a.reshape(a.shape[0], 2, N_FF_CHUNKS, FF_CHUNK)
    wup = chunks(w_up).transpose(2, 0, 1, 3).reshape(N_FF_CHUNKS, D_MODEL, 2 * FF_CHUNK).astype(BF16)
    cw = chunks(conv_w).transpose(2, 0, 1, 3).reshape(N_FF_CHUNKS, 3, 2 * FF_CHUNK)
    cb = chunks(conv_b[None]).transpose(2, 0, 1, 3).reshape(N_FF_CHUNKS, 1, 2 * FF_CHUNK)
    wd = w_down.reshape(N_FF_CHUNKS, FF_CHUNK, D_MODEL).astype(BF16)
    return dict(w1=w1, wgt=wgt, wg=wg, bg=b_gate[None], wuqt=wuqt, wuk=wuk, wuvt=wuvt, wbr=wbr,
                wo=w_out.astype(BF16), wup=wup, cw=cw, cb=cb, wd=wd)


def _tiles(seq):
    tm = min(512, seq)
    tq = min(1024, seq)
    return tm, tq


def kernel(x, norm1_g, w_in, b_gate, na_rpb, mla_qa_g, mla_kva_g, mla_w_uq, mla_w_ukv, gqa_sink, w_br_na, w_br_mla, w_br_gqa, w_out, norm2_g, w_up, conv_w, conv_b, w_down, final_g):
    batch, seq, _ = x.shape
    depth = w_in.shape[0]
    rows = seq // GRID_W
    assert seq % (NA_ROWS_PER_STEP * GRID_W) == 0 and rows >= 12 and rows % 2 == 0
    assert seq % (GQA_BLOCKS_PER_STEP * BLOCK) == 0 and MLA_ROPE == GQA_HEAD_DIM
    tm, tq = _tiles(seq)
    assert tm % MLA_KC == 0 and seq % tq == 0
    rope, rope_t = _rope_tables(seq)
    gqa_bias = _gqa_mask_bias(seq // BLOCK)
    xf = x.reshape(batch * seq, D_MODEL)
    for l in range(depth):
        p = _prep_layer(w_in[l], b_gate[l], mla_w_uq[l], mla_w_ukv[l], w_br_na[l], w_br_mla[l],
                        w_br_gqa[l], w_out[l], w_up[l], conv_w[l], conv_b[l], w_down[l])
        g1 = norm1_g[l][None]
        naqt, nak, navt, mqt, mk, mvt, gqt, gk, gvt = _inproj(
            xf, g1, p["w1"], mla_qa_g[l][None], mla_kva_g[l][None], p["wuqt"], p["wuk"], p["wuvt"],
            p["wgt"], rope, rope_t, seq=seq, tm=tm)
        y_na_t = _na(naqt, nak, navt, _na_bias_table(na_rpb[l], rows), batch=batch, seq=seq)
        y_mla = _mla(mqt, mk, mvt, batch=batch, seq=seq, tq=tq).reshape(batch * seq, -1)
        sink = jnp.repeat(gqa_sink[l].astype(F32).reshape(GQA_KV_HEADS, 1, GQA_GROUP), BLOCK, axis=-1)
        y_gqa_t = _gqa(gqt, gk, gvt, gqa_bias, sink, batch=batch, seq=seq)
        xf = _mixout(xf, y_na_t, y_mla, y_gqa_t, g1, p["wg"], p["bg"], p["wbr"], p["wo"], tm=tm)
        xf = _ffn(xf, norm2_g[l][None], p["wup"], p["cw"], p["cb"], p["wd"], final_g[None],
                  seq=seq, tm=tm, final=(l == depth - 1))
    return xf.reshape(batch, seq, D_MODEL)
```

```python
import functools

import jax
import jax.numpy as jnp
import numpy as np
from jax import lax
from jax.experimental import pallas as pl
from jax.experimental.pallas import tpu as pltpu

F32 = jnp.float32
BF16 = jnp.bfloat16

D_MODEL = 1024
GRID_W = 64
NA_HEADS = 8
NA_HEAD_DIM = 64
NA_WIN_R = 8
NA_WIN_C = 16
NA_W = NA_HEADS * NA_HEAD_DIM
MLA_HEADS = 4
MLA_Q_RANK = 384
MLA_KV_RANK = 256
MLA_NOPE = 128
MLA_ROPE = 64
MLA_V = 128
MLA_QK = MLA_NOPE + MLA_ROPE
MLA_QK_PAD = 256
MLA_KC = 512
MLA_VT_ROWS = MLA_V + 16
MLA_LOG2_SCALE = float(MLA_QK ** -0.5 * np.log2(np.e))
GQA_HEADS = 8
GQA_KV_HEADS = 2
GQA_GROUP = GQA_HEADS // GQA_KV_HEADS
GQA_HEAD_DIM = 64
GQA_WINDOW = 128
GQA_Q_W = GQA_HEADS * GQA_HEAD_DIM
GQA_KV_W = GQA_KV_HEADS * GQA_HEAD_DIM
BLOCK = 128
N_BRANCH = 3
D_FF = 2816
ROPE_THETA = 10000.0
EPS = 1e-6
NEG_INF = -1e30

LANES = 128
ONES_ROWS = 16
IN1_W = 1408
IN2_W = 1664
FF_CHUNK = 256
N_FF_CHUNKS = D_FF // FF_CHUNK
FF_UNROLL = True
HALO = 16
VMEM_LIMIT = 56 * 1024 * 1024

_O_NAK = 0
_O_CQ = 512
_O_CKV = 896
_O_GK = 1152
_O_KR = 1280
_T_NAQ = 0
_T_NAV = 512
_T_GQ = 1024
_T_GV = 1536


def _rms(x, g):
    return x * lax.rsqrt(jnp.mean(x * x, axis=-1, keepdims=True) + EPS) * g


def _rope128(blk, c, s1, s2):
    return blk * c + pltpu.roll(blk, 32, 1) * s1 + pltpu.roll(blk, 96, 1) * s2


def _dot(a, b):
    return jnp.dot(a, b, preferred_element_type=F32)


def _dot_nt(a, b):
    return lax.dot_general(a, b, (((1,), (1,)), ((), ())), preferred_element_type=F32)


def _dot_tn(a, b):
    return lax.dot_general(a, b, (((0,), (0,)), ((), ())), preferred_element_type=F32)


def _ones_rows(n):
    return (lax.broadcasted_iota(jnp.int32, (ONES_ROWS, n), 0) == 0).astype(BF16)


def _params(n_axes):
    return pltpu.CompilerParams(dimension_semantics=("arbitrary",) * n_axes,
                                vmem_limit_bytes=VMEM_LIMIT)


def _const_spec(shape):
    n = len(shape)
    return pl.BlockSpec(shape, lambda *_: (0,) * n, pipeline_mode=pl.Buffered(1))


def _inproj_kernel(x_ref, g1_ref, w1_ref, gqa_ref, gkva_ref, wuqt_ref, wuk_ref, wuvt_ref, wgt_ref,
                   rope_ref, ropet_ref, naqt_ref, nak_ref, navt_ref, mqt_ref, mk_ref, mvt_ref,
                   gqt_ref, gk_ref, gvt_ref):
    tm = x_ref.shape[0]
    h = _rms(x_ref[...], g1_ref[...]).astype(BF16)
    z = _dot(h, w1_ref[...])
    gt = _dot_nt(wgt_ref[...], h)

    naqt_ref[...] = gt[_T_NAQ:_T_NAV, :].astype(BF16)
    for c in range(tm // LANES):
        navt_ref[c] = gt[_T_NAV:_T_GQ, c * LANES:(c + 1) * LANES].astype(BF16)
    nak_ref[...] = z[:, _O_NAK:_O_CQ].astype(BF16)

    mc, ms1, ms2 = rope_ref[:, 0:128], rope_ref[:, 128:256], rope_ref[:, 256:384]
    gc, gs1, gs2 = rope_ref[:, 384:512], rope_ref[:, 512:640], rope_ref[:, 640:768]

    cqn = _rms(z[:, _O_CQ:_O_CKV], gqa_ref[...]).astype(BF16)
    qt = _dot_nt(wuqt_ref[...], cqn) * MLA_LOG2_SCALE
    half = MLA_ROPE // 2
    ct, st = ropet_ref[0:half, :], ropet_ref[half:MLA_ROPE, :]
    for hd in range(MLA_HEADS):
        o = hd * MLA_QK_PAD
        r = o + MLA_NOPE
        mqt_ref[o:r, :] = qt[o:r, :].astype(BF16)
        x1, x2 = qt[r:r + half, :], qt[r + half:r + MLA_ROPE, :]
        mqt_ref[r:r + half, :] = (x1 * ct - x2 * st).astype(BF16)
        mqt_ref[r + half:r + MLA_ROPE, :] = (x1 * st + x2 * ct).astype(BF16)
        mqt_ref[r + MLA_ROPE:o + MLA_QK_PAD, :] = qt[r + MLA_ROPE:o + MLA_QK_PAD, :].astype(BF16)

    ckvn = _rms(z[:, _O_CKV:_O_GK], gkva_ref[...]).astype(BF16)
    kn = _dot(ckvn, wuk_ref[...])
    kpe = _rope128(z[:, _O_KR:_O_KR + LANES], mc, ms1, ms2).astype(BF16)
    for hd in range(MLA_HEADS):
        o = hd * MLA_QK_PAD
        mk_ref[:, o:o + LANES] = kn[:, hd * LANES:(hd + 1) * LANES].astype(BF16)
        mk_ref[:, o + LANES:o + 2 * LANES] = kpe
    vt = _dot_nt(wuvt_ref[...], ckvn)
    ones_rows = _ones_rows(MLA_KC)
    for hd in range(MLA_HEADS):
        for c in range(tm // MLA_KC):
            mvt_ref[hd, c, 0:MLA_V, :] = vt[hd * MLA_V:(hd + 1) * MLA_V,
                                            c * MLA_KC:(c + 1) * MLA_KC].astype(BF16)
            mvt_ref[hd, c, MLA_V:MLA_VT_ROWS, :] = ones_rows

    for hd in range(GQA_HEADS):
        r = _T_GQ + hd * GQA_HEAD_DIM
        x1, x2 = gt[r:r + half, :], gt[r + half:r + GQA_HEAD_DIM, :]
        o = hd * GQA_HEAD_DIM
        gqt_ref[o:o + half, :] = (x1 * ct - x2 * st).astype(BF16)
        gqt_ref[o + half:o + GQA_HEAD_DIM, :] = (x1 * st + x2 * ct).astype(BF16)
    gvt_ref[...] = gt[_T_GV:, :].astype(BF16)
    gk_ref[...] = _rope128(z[:, _O_GK:_O_KR], gc, gs1, gs2).astype(BF16)


def _inproj(x2d, g1, w1, g_qa, g_kva, wuqt, wuk, wuvt, wgt, rope, ropet, *, seq, tm):
    T = x2d.shape[0]
    tiles_per_seq = seq // tm
    cpt = tm // MLA_KC
    row = lambda w: pl.BlockSpec((tm, w), lambda i: (i, 0))
    col = lambda w: pl.BlockSpec((w, tm), lambda i: (0, i))
    qk_w = MLA_HEADS * MLA_QK_PAD
    out_specs = [col(NA_W), row(NA_W),
                 pl.BlockSpec((tm // LANES, NA_W, LANES), lambda i: (i, 0, 0)),
                 col(qk_w), row(qk_w),
                 pl.BlockSpec((MLA_HEADS, cpt, MLA_VT_ROWS, MLA_KC), lambda i: (0, i, 0, 0)),
                 col(GQA_Q_W), row(GQA_KV_W), col(GQA_KV_W)]
    out_shape = [jax.ShapeDtypeStruct((NA_W, T), BF16),
                 jax.ShapeDtypeStruct((T, NA_W), BF16),
                 jax.ShapeDtypeStruct((T // LANES, NA_W, LANES), BF16),
                 jax.ShapeDtypeStruct((qk_w, T), BF16),
                 jax.ShapeDtypeStruct((T, qk_w), BF16),
                 jax.ShapeDtypeStruct((MLA_HEADS, T // MLA_KC, MLA_VT_ROWS, MLA_KC), BF16),
                 jax.ShapeDtypeStruct((GQA_Q_W, T), BF16),
                 jax.ShapeDtypeStruct((T, GQA_KV_W), BF16),
                 jax.ShapeDtypeStruct((GQA_KV_W, T), BF16)]
    return pl.pallas_call(
        _inproj_kernel,
        grid=(T // tm,),
        in_specs=[row(D_MODEL), _const_spec((1, D_MODEL)), _const_spec((D_MODEL, IN1_W)),
                  _const_spec((1, MLA_Q_RANK)), _const_spec((1, MLA_KV_RANK)),
                  _const_spec((qk_w, MLA_Q_RANK)),
                  _const_spec((MLA_KV_RANK, MLA_HEADS * MLA_NOPE)),
                  _const_spec((MLA_HEADS * MLA_V, MLA_KV_RANK)),
                  _const_spec((IN2_W, D_MODEL)),
                  pl.BlockSpec((tm, 6 * LANES), lambda i: (i % tiles_per_seq, 0)),
                  pl.BlockSpec((MLA_ROPE, tm), lambda i: (0, i % tiles_per_seq))],
        out_specs=out_specs,
        out_shape=out_shape,
        compiler_params=_params(1),
        name="inproj",
    )(x2d, g1, w1, g_qa, g_kva, wuqt, wuk, wuvt, wgt, rope, ropet)


NA_ROWS_PER_STEP = 8
NA_KEY_ROWS = 10
NA_CASES = 5


def _na_kernel(qt_ref, k_ref, vt_ref, tb_ref, o_ref, *, rows):
    step = pl.program_id(1)
    n_keys = NA_KEY_ROWS * GRID_W
    ones_rows = _ones_rows(n_keys)
    zeros = jnp.zeros((NA_HEAD_DIM, LANES), BF16)
    pairs = []
    for rp in range(NA_ROWS_PER_STEP // 2):
        r = step * NA_ROWS_PER_STEP + 2 * rp
        case = jnp.where(r == 0, 0, jnp.where(r == 2, 1, jnp.where(
            r == rows - 4, 3, jnp.where(r == rows - 2, 4, 2))))
        u = jnp.clip(r - NA_WIN_R // 2, 0, rows - NA_KEY_ROWS)
        pairs.append((case, pl.multiple_of(u * GRID_W, LANES), u // 2))

    def scores(rp, hp):
        case, tok0, _ = pairs[rp]
        lo, mid, hi = hp * LANES, hp * LANES + NA_HEAD_DIM, (hp + 1) * LANES
        qcols = slice(rp * LANES, (rp + 1) * LANES)
        qa, qb = qt_ref[lo:mid, qcols], qt_ref[mid:hi, qcols]
        rhs = jnp.concatenate([jnp.concatenate([qa, zeros], axis=1),
                               jnp.concatenate([zeros, qb], axis=1)], axis=0)
        bias = jnp.concatenate([tb_ref[case, 2 * hp], tb_ref[case, 2 * hp + 1]], axis=1)
        return _dot(k_ref[pl.ds(tok0, n_keys), lo:hi], rhs) + bias

    def finish(rp, hp, t):
        _, _, c0 = pairs[rp]
        lo, mid, hi = hp * LANES, hp * LANES + NA_HEAD_DIM, (hp + 1) * LANES
        qcols = slice(rp * LANES, (rp + 1) * LANES)
        m = jnp.max(t, axis=0, keepdims=True)
        p = jnp.exp(t - m).astype(BF16)
        vt_win = jnp.concatenate([vt_ref[c0 + j, lo:hi, :] for j in range(n_keys // LANES)], axis=1)
        lhs = jnp.concatenate([vt_win, ones_rows], axis=0)
        acc = _dot(lhs, p)
        l = acc[LANES:LANES + 1, :]
        o_ref[lo:mid, qcols] = (acc[0:NA_HEAD_DIM, 0:LANES] / l[:, 0:LANES]).astype(BF16)
        o_ref[mid:hi, qcols] = (acc[NA_HEAD_DIM:LANES, LANES:] / l[:, LANES:]).astype(BF16)

    units = [(rp, hp) for rp in range(NA_ROWS_PER_STEP // 2) for hp in range(NA_HEADS // 2)]
    t = scores(*units[0])
    for idx, unit in enumerate(units):
        t_next = scores(*units[idx + 1]) if idx + 1 < len(units) else None
        finish(*unit, t)
        t = t_next


def _na(naqt, nak, navt, tb, *, batch, seq):
    rows = seq // GRID_W
    tq = NA_ROWS_PER_STEP * GRID_W
    n_steps = seq // tq
    T = batch * seq
    k3 = nak.reshape(batch, seq, NA_W)
    vt4 = navt.reshape(batch, seq // LANES, NA_W, LANES)
    return pl.pallas_call(
        functools.partial(_na_kernel, rows=rows),
        grid=(batch, n_steps),
        in_specs=[pl.BlockSpec((NA_W, tq), lambda b, i: (0, b * n_steps + i)),
                  pl.BlockSpec((None, seq, NA_W), lambda b, i: (b, 0, 0), pipeline_mode=pl.Buffered(1)),
                  pl.BlockSpec((None, seq // LANES, NA_W, LANES), lambda b, i: (b, 0, 0, 0),
                               pipeline_mode=pl.Buffered(1)),
                  _const_spec(tb.shape)],
        out_specs=pl.BlockSpec((NA_W, tq), lambda b, i: (0, b * n_steps + i)),
        out_shape=jax.ShapeDtypeStruct((NA_W, T), BF16),
        compiler_params=_params(2),
        name="na",
    )(naqt, k3, vt4, tb)


def _na_bias_table(rpb, rows):
    kc = np.arange(GRID_W)[:, None]
    c = np.arange(GRID_W)[None, :]
    c_start = np.clip(c - NA_WIN_C // 2, 0, GRID_W - NA_WIN_C)
    valid_c = (kc >= c_start) & (kc < c_start + NA_WIN_C)
    side = GRID_W - NA_WIN_C
    padded = jnp.pad(rpb.astype(F32), ((0, 0), (0, 0), (side, side)))
    toe = jnp.stack([padded[:, :, GRID_W - 1 - cc:2 * GRID_W - 1 - cc] for cc in range(GRID_W)], axis=-1)
    toe = jnp.where(valid_c[None, None], toe, NEG_INF)
    neg = jnp.full((rpb.shape[0], GRID_W, GRID_W), NEG_INF, F32)
    tabs = []
    for r in (0, 2, 4, rows - 4, rows - 2):
        u = min(max(r - NA_WIN_R // 2, 0), rows - NA_KEY_ROWS)
        key_rows = []
        for i in range(NA_KEY_ROWS):
            blocks = []
            for rr in range(2):
                r_start = min(max(r + rr - NA_WIN_R // 2, 0), rows - NA_WIN_R)
                inside = r_start <= u + i < r_start + NA_WIN_R
                blocks.append(toe[:, u + i - (r + rr) + (NA_WIN_R - 1)] if inside else neg)
            key_rows.append(jnp.concatenate(blocks, axis=-1))
        tabs.append(jnp.concatenate(key_rows, axis=1))
    return jnp.stack(tabs)


def _mla_kernel(qt_ref, k_ref, vt_ref, o_ref, *, n_chunks):
    qt = qt_ref[...]
    tq = qt.shape[1]

    def scores(c):
        k0 = pl.multiple_of(c * MLA_KC, MLA_KC)
        return _dot(k_ref[pl.ds(k0, MLA_KC), :], qt)

    def update(c, m, acc, t):
        m_new = jnp.maximum(m, jnp.max(t, axis=0, keepdims=True))
        alpha = jnp.exp2(m - m_new)
        p = jnp.exp2(t - m_new).astype(BF16)
        return m_new, alpha * acc + _dot(vt_ref[c], p)

    def body(c, carry):
        m, acc, t = carry
        t_next = scores(c + 1)
        m, acc = update(c, m, acc, t)
        return m, acc, t_next

    m0 = jnp.full((1, tq), -jnp.inf, F32)
    acc0 = jnp.zeros((MLA_VT_ROWS, tq), F32)
    m, acc, t = lax.fori_loop(0, n_chunks - 1, body, (m0, acc0, scores(0)), unroll=True)
    _, acc = update(n_chunks - 1, m, acc, t)
    o = acc[0:MLA_V, :] / acc[MLA_V:MLA_V + 1, :]
    o_ref[...] = o.T.astype(BF16)


def _mla(mqt, mk, mvt, *, batch, seq, tq):
    n_chunks = seq // MLA_KC
    nq = seq // tq
    k3 = mk.reshape(batch, seq, MLA_HEADS * MLA_QK_PAD)
    return pl.pallas_call(
        functools.partial(_mla_kernel, n_chunks=n_chunks),
        grid=(batch, MLA_HEADS, nq),
        in_specs=[pl.BlockSpec((MLA_QK_PAD, tq), lambda b, h, i: (h, b * nq + i)),
                  pl.BlockSpec((None, seq, MLA_QK_PAD), lambda b, h, i: (b, 0, h)),
                  pl.BlockSpec((None, n_chunks, MLA_VT_ROWS, MLA_KC), lambda b, h, i: (h, b, 0, 0))],
        out_specs=pl.BlockSpec((None, tq, MLA_V), lambda b, h, i: (b, i, h)),
        out_shape=jax.ShapeDtypeStruct((batch, seq, MLA_HEADS * MLA_V), BF16),
        compiler_params=_params(3),
        name="mla",
    )(mqt, k3, mvt)


GQA_BLOCKS_PER_STEP = 4


def _gqa_kernel(qt_ref, kp_ref, kc_ref, kn_ref, vp_ref, vc_ref, vn_ref, b_ref, sk_ref, o_ref, *,
                n_steps):
    i = pl.program_id(1)
    nbs = GQA_BLOCKS_PER_STEP
    k_all = jnp.concatenate([kp_ref[...], kc_ref[...], kn_ref[...]], axis=0)
    vt_all = jnp.concatenate([vp_ref[...], vc_ref[...], vn_ref[...]], axis=1)
    ones_rows = _ones_rows(3 * BLOCK)
    zeros = jnp.zeros((GQA_HEAD_DIM, GQA_GROUP * BLOCK), BF16)

    def scores(j, kvh):
        if j == 0:
            bias = b_ref[jnp.where(i == 0, 0, 1)]
        elif j == nbs - 1:
            bias = b_ref[jnp.where(i == n_steps - 1, 2, 1)]
        else:
            bias = b_ref[1]
        bias = jnp.concatenate([bias] * GQA_GROUP, axis=1)
        heads = range(kvh * GQA_GROUP, (kvh + 1) * GQA_GROUP)
        qs = jnp.concatenate([qt_ref[hd * GQA_HEAD_DIM:(hd + 1) * GQA_HEAD_DIM,
                                     j * BLOCK:(j + 1) * BLOCK] for hd in heads], axis=1)
        rhs = jnp.concatenate([qs, zeros] if kvh == 0 else [zeros, qs], axis=0)
        return _dot(k_all[j * BLOCK:(j + 3) * BLOCK, :], rhs) + bias

    def finish(j, kvh, t):
        sink = sk_ref[kvh]
        m = jnp.maximum(jnp.max(t, axis=0, keepdims=True), sink)
        p = jnp.exp(t - m).astype(BF16)
        vt2 = vt_all[kvh * GQA_HEAD_DIM:(kvh + 1) * GQA_HEAD_DIM, j * BLOCK:(j + 3) * BLOCK]
        acc = _dot(jnp.concatenate([vt2, ones_rows], axis=0), p)
        l = acc[GQA_HEAD_DIM:GQA_HEAD_DIM + 1, :] + jnp.exp(sink - m)
        o = acc[0:GQA_HEAD_DIM, :] / l
        for g in range(GQA_GROUP):
            hd = kvh * GQA_GROUP + g
            o_ref[hd * GQA_HEAD_DIM:(hd + 1) * GQA_HEAD_DIM, j * BLOCK:(j + 1) * BLOCK] = (
                o[:, g * BLOCK:(g + 1) * BLOCK].astype(BF16))

    units = [(j, kvh) for j in range(nbs) for kvh in range(GQA_KV_HEADS)]
    t = scores(*units[0])
    for idx, unit in enumerate(units):
        t_next = scores(*units[idx + 1]) if idx + 1 < len(units) else None
        finish(*unit, t)
        t = t_next


def _gqa(gqt, gk, gvt, bias3, sink, *, batch, seq):
    nb = seq // BLOCK
    nbs = GQA_BLOCKS_PER_STEP
    n_steps = nb // nbs
    T = batch * seq
    k3 = gk.reshape(batch, seq, GQA_KV_W)
    prev = lambda i: jnp.maximum(i * nbs - 1, 0)
    nxt = lambda i: jnp.minimum((i + 1) * nbs, nb - 1)
    return pl.pallas_call(
        functools.partial(_gqa_kernel, n_steps=n_steps),
        grid=(batch, n_steps),
        in_specs=[pl.BlockSpec((GQA_Q_W, nbs * BLOCK), lambda b, i: (0, b * n_steps + i)),
                  pl.BlockSpec((None, BLOCK, GQA_KV_W), lambda b, i: (b, prev(i), 0)),
                  pl.BlockSpec((None, nbs * BLOCK, GQA_KV_W), lambda b, i: (b, i, 0)),
                  pl.BlockSpec((None, BLOCK, GQA_KV_W), lambda b, i: (b, nxt(i), 0)),
                  pl.BlockSpec((GQA_KV_W, BLOCK), lambda b, i: (0, b * nb + prev(i))),
                  pl.BlockSpec((GQA_KV_W, nbs * BLOCK), lambda b, i: (0, b * n_steps + i)),
                  pl.BlockSpec((GQA_KV_W, BLOCK), lambda b, i: (0, b * nb + nxt(i))),
                  _const_spec((3, 3 * BLOCK, BLOCK)),
                  _const_spec((GQA_KV_HEADS, 1, GQA_GROUP * BLOCK))],
        out_specs=pl.BlockSpec((GQA_Q_W, nbs * BLOCK), lambda b, i: (0, b * n_steps + i)),
        out_shape=jax.ShapeDtypeStruct((GQA_Q_W, T), BF16),
        compiler_params=_params(2),
        name="gqa",
    )(gqt, k3, k3, k3, gvt, gvt, gvt, bias3, sink)


def _gqa_mask_bias(nb):
    i = np.arange(BLOCK)[None, :]
    j = np.arange(3 * BLOCK)[:, None]
    band = np.abs(BLOCK + i - j) <= GQA_WINDOW
    first = band & (j >= BLOCK)
    last = band & (j < 2 * BLOCK)
    if nb == 1:
        first = last = first & last
    m = np.stack([first, band, last])
    return jnp.asarray(np.where(m, 0.0, NEG_INF), F32)


def _mixout_kernel(x_ref, ynat_ref, ymla_ref, ygqat_ref, g1_ref, wg_ref, bg_ref, wbr_ref, wo_ref, o_ref):
    x = x_ref[...]
    h = _rms(x, g1_ref[...]).astype(BF16)
    branches = (_dot_tn(ynat_ref[...], wbr_ref[0]), _dot(ymla_ref[...], wbr_ref[1]),
                _dot_tn(ygqat_ref[...], wbr_ref[2]))
    merged = None
    for i, br in enumerate(branches):
        cols = slice(i * D_MODEL, (i + 1) * D_MODEL)
        gate = jax.nn.sigmoid(_dot(h, wg_ref[:, cols]) + bg_ref[:, cols])
        merged = gate * br if merged is None else merged + gate * br
    o_ref[...] = x + _dot(merged.astype(BF16), wo_ref[...])


def _mixout(x2d, y_na_t, y_mla, y_gqa_t, g1, wg, bg, wbr, wo, *, tm):
    T = x2d.shape[0]
    row = lambda w: pl.BlockSpec((tm, w), lambda i: (i, 0))
    col = lambda w: pl.BlockSpec((w, tm), lambda i: (0, i))
    return pl.pallas_call(
        _mixout_kernel,
        grid=(T // tm,),
        in_specs=[row(D_MODEL), col(NA_W), row(MLA_HEADS * MLA_V), col(GQA_Q_W),
                  _const_spec((1, D_MODEL)), _const_spec((D_MODEL, N_BRANCH * D_MODEL)),
                  _const_spec((1, N_BRANCH * D_MODEL)), _const_spec((N_BRANCH, NA_W, D_MODEL)),
                  _const_spec((D_MODEL, D_MODEL))],
        out_specs=row(D_MODEL),
        out_shape=jax.ShapeDtypeStruct((T, D_MODEL), F32),
        compiler_params=_params(1),
        name="mixout",
    )(x2d, y_na_t, y_mla, y_gqa_t, g1, wg, bg, wbr, wo)


def _ffn_kernel(x_ref, xp_ref, xn_ref, g2_ref, wup_ref, cw_ref, cb_ref, wd_ref, fg_ref, o_ref,
                hx_ref, u_ref, acc_ref, *, tm, tiles_per_seq, final):
    j = pl.program_id(0) % tiles_per_seq
    g2 = g2_ref[...]
    x = x_ref[...]
    hp = jnp.where(j == 0, 0.0, _rms(xp_ref[...], g2))
    hn = jnp.where(j == tiles_per_seq - 1, 0.0, _rms(xn_ref[...], g2))
    hx_ref[0:HALO, :] = hp.astype(BF16)
    hx_ref[HALO:HALO + tm, :] = _rms(x, g2).astype(BF16)
    hx_ref[HALO + tm:, :] = hn.astype(BF16)

    def up(c):
        u_ref[c % 2] = _dot(hx_ref[...], wup_ref[c])

    def down(c):
        ub = u_ref.at[c % 2]
        cw = cw_ref[c]
        y = (cw[0:1] * ub[pl.ds(HALO - 1, tm), :] + cw[1:2] * ub[pl.ds(HALO, tm), :]
             + cw[2:3] * ub[pl.ds(HALO + 1, tm), :] + cb_ref[c])
        act = jax.nn.gelu(y[:, FF_CHUNK:]) * y[:, :FF_CHUNK]
        acc_ref[...] += _dot(act.astype(BF16), wd_ref[c])

    acc_ref[...] = jnp.zeros(acc_ref.shape, F32)
    up(0)

    def body(c, carry):
        up(c + 1)
        down(c)
        return carry

    lax.fori_loop(0, N_FF_CHUNKS - 1, body, 0, unroll=FF_UNROLL)
    down(N_FF_CHUNKS - 1)
    out = x + acc_ref[...]
    if final:
        out = _rms(out, fg_ref[...])
    o_ref[...] = out


def _ffn(x2d, g2, wup, cw, cb, wd, fg, *, seq, tm, final):
    T = x2d.shape[0]
    tiles_per_seq = seq // tm
    hb = tm // HALO
    n_hb = T // HALO
    return pl.pallas_call(
        functools.partial(_ffn_kernel, tm=tm, tiles_per_seq=tiles_per_seq, final=final),
        grid=(T // tm,),
        in_specs=[pl.BlockSpec((tm, D_MODEL), lambda i: (i, 0)),
                  pl.BlockSpec((HALO, D_MODEL), lambda i: (jnp.maximum(i * hb - 1, 0), 0)),
                  pl.BlockSpec((HALO, D_MODEL), lambda i: (jnp.minimum((i + 1) * hb, n_hb - 1), 0)),
                  _const_spec((1, D_MODEL)),
                  _const_spec((N_FF_CHUNKS, D_MODEL, 2 * FF_CHUNK)),
                  _const_spec((N_FF_CHUNKS, 3, 2 * FF_CHUNK)),
                  _const_spec((N_FF_CHUNKS, 1, 2 * FF_CHUNK)),
                  _const_spec((N_FF_CHUNKS, FF_CHUNK, D_MODEL)),
                  _const_spec((1, D_MODEL))],
        out_specs=pl.BlockSpec((tm, D_MODEL), lambda i: (i, 0)),
        out_shape=jax.ShapeDtypeStruct((T, D_MODEL), F32),
        scratch_shapes=[pltpu.VMEM((tm + 2 * HALO, D_MODEL), BF16),
                        pltpu.VMEM((2, tm + 2 * HALO, 2 * FF_CHUNK), F32),
                        pltpu.VMEM((tm, D_MODEL), F32)],
        compiler_params=_params(1),
        name="ffn",
    )(x2d, x2d, x2d, g2, wup, cw, cb, wd, fg)


def _rope_tables(seq):
    def cs(dim):
        inv = 1.0 / (ROPE_THETA ** (jnp.arange(0, dim, 2, dtype=F32) / dim))
        ang = jnp.arange(seq, dtype=F32)[:, None] * inv[None, :]
        return jnp.cos(ang), jnp.sin(ang)

    def head(c, s):
        z = jnp.zeros_like(s)
        return (jnp.concatenate([c, c], -1), jnp.concatenate([z, s], -1), jnp.concatenate([-s, z], -1))

    m_cos, m_sin = cs(MLA_ROPE)
    mc, ms1, ms2 = head(m_cos, m_sin)
    pad = jnp.zeros_like(mc)
    gc, gs1, gs2 = head(*cs(GQA_HEAD_DIM))
    two = lambda a: jnp.concatenate([a, a], -1)
    table = jnp.concatenate([mc, pad, ms1, pad, ms2, pad, two(gc), two(gs1), two(gs2)], axis=-1)
    table_t = jnp.concatenate([m_cos, m_sin], axis=-1).T
    return table, table_t


def _prep_layer(w_in, b_gate, mla_w_uq, mla_w_ukv, w_br_na, w_br_mla, w_br_gqa, w_out,
                w_up, conv_w, conv_b, w_down):
    o_nak = NA_W
    o_nav = 2 * NA_W
    o_cq = 3 * NA_W
    o_ckv = o_cq + MLA_Q_RANK
    o_kr = o_ckv + MLA_KV_RANK
    o_gq = o_kr + MLA_ROPE
    o_gkv = o_gq + GQA_Q_W
    o_gv = o_gkv + GQA_KV_W
    o_gate = o_gv + GQA_KV_W
    na_scale = NA_HEAD_DIM ** -0.5
    gqa_scale = GQA_HEAD_DIM ** -0.5
    w1 = jnp.concatenate([
        w_in[:, o_nak:o_nav], w_in[:, o_cq:o_ckv], w_in[:, o_ckv:o_kr], w_in[:, o_gkv:o_gv],
        w_in[:, o_kr:o_gq], jnp.zeros((D_MODEL, LANES - MLA_ROPE), w_in.dtype)], axis=1).astype(BF16)
    wgt = jnp.concatenate([w_in[:, 0:o_nak] * na_scale, w_in[:, o_nav:o_cq],
                           w_in[:, o_gq:o_gkv] * gqa_scale, w_in[:, o_gv:o_gate]], axis=1).T.astype(BF16)
    wg = w_in[:, o_gate:].astype(BF16)
    wuqt = jnp.pad(mla_w_uq.reshape(MLA_Q_RANK, MLA_HEADS, MLA_QK),
                   ((0, 0), (0, 0), (0, MLA_QK_PAD - MLA_QK))).reshape(MLA_Q_RANK, -1).T.astype(BF16)
    ukv = mla_w_ukv.reshape(MLA_KV_RANK, MLA_HEADS, MLA_NOPE + MLA_V)
    wuk = ukv[:, :, :MLA_NOPE].reshape(MLA_KV_RANK, -1).astype(BF16)
    wuvt = ukv[:, :, MLA_NOPE:].reshape(MLA_KV_RANK, -1).T.astype(BF16)
    wbr = jnp.stack([w_br_na, w_br_mla, w_br_gqa]).astype(BF16)
    chunks = lambda a: a.reshape(a.shape[0], 2, N_FF_CHUNKS, FF_CHUNK)
    wup = chunks(w_up).transpose(2, 0, 1, 3).reshape(N_FF_CHUNKS, D_MODEL, 2 * FF_CHUNK).astype(BF16)
    cw = chunks(conv_w).transpose(2, 0, 1, 3).reshape(N_FF_CHUNKS, 3, 2 * FF_CHUNK)
    cb = chunks(conv_b[None]).transpose(2, 0, 1, 3).reshape(N_FF_CHUNKS, 1, 2 * FF_CHUNK)
    wd = w_down.reshape(N_FF_CHUNKS, FF_CHUNK, D_MODEL).astype(BF16)
    return dict(w1=w1, wgt=wgt, wg=wg, bg=b_gate[None], wuqt=wuqt, wuk=wuk, wuvt=wuvt, wbr=wbr,
                wo=w_out.astype(BF16), wup=wup, cw=cw, cb=cb, wd=wd)


def _tiles(seq):
    tm = min(512, seq)
    tq = min(2048, seq)
    return tm, tq


def kernel(x, norm1_g, w_in, b_gate, na_rpb, mla_qa_g, mla_kva_g, mla_w_uq, mla_w_ukv, gqa_sink, w_br_na, w_br_mla, w_br_gqa, w_out, norm2_g, w_up, conv_w, conv_b, w_down, final_g):
    batch, seq, _ = x.shape
    depth = w_in.shape[0]
    rows = seq // GRID_W
    assert seq % (NA_ROWS_PER_STEP * GRID_W) == 0 and rows >= 12 and rows % 2 == 0
    assert seq % (GQA_BLOCKS_PER_STEP * BLOCK) == 0 and MLA_ROPE == GQA_HEAD_DIM
    tm, tq = _tiles(seq)
    assert tm % MLA_KC == 0 and seq % tq == 0
    rope, rope_t = _rope_tables(seq)
    gqa_bias = _gqa_mask_bias(seq // BLOCK)
    xf = x.reshape(batch * seq, D_MODEL)
    for l in range(depth):
        p = _prep_layer(w_in[l], b_gate[l], mla_w_uq[l], mla_w_ukv[l], w_br_na[l], w_br_mla[l],
                        w_br_gqa[l], w_out[l], w_up[l], conv_w[l], conv_b[l], w_down[l])
        g1 = norm1_g[l][None]
        naqt, nak, navt, mqt, mk, mvt, gqt, gk, gvt = _inproj(
            xf, g1, p["w1"], mla_qa_g[l][None], mla_kva_g[l][None], p["wuqt"], p["wuk"], p["wuvt"],
            p["wgt"], rope, rope_t, seq=seq, tm=tm)
        y_na_t = _na(naqt, nak, navt, _na_bias_table(na_rpb[l], rows), batch=batch, seq=seq)
        y_mla = _mla(mqt, mk, mvt, batch=batch, seq=seq, tq=tq).reshape(batch * seq, -1)
        sink = jnp.repeat(gqa_sink[l].astype(F32).reshape(GQA_KV_HEADS, 1, GQA_GROUP), BLOCK, axis=-1)
        y_gqa_t = _gqa(gqt, gk, gvt, gqa_bias, sink, batch=batch, seq=seq)
        xf = _mixout(xf, y_na_t, y_mla, y_gqa_t, g1, p["wg"], p["bg"], p["wbr"], p["wo"], tm=tm)
        xf = _ffn(xf, norm2_g[l][None], p["wup"], p["cw"], p["cb"], p["wd"], final_g[None],
                  seq=seq, tm=tm, final=(l == depth - 1))
    return xf.reshape(batch, seq, D_MODEL)
```

```python
import functools

import jax
import jax.numpy as jnp
import numpy as np
from jax import lax
from jax.experimental import pallas as pl
from jax.experimental.pallas import tpu as pltpu

F32 = jnp.float32
BF16 = jnp.bfloat16

D_MODEL = 1024
GRID_W = 64
NA_HEADS = 8
NA_HEAD_DIM = 64
NA_WIN_R = 8
NA_WIN_C = 16
NA_W = NA_HEADS * NA_HEAD_DIM
MLA_HEADS = 4
MLA_Q_RANK = 384
MLA_KV_RANK = 256
MLA_NOPE = 128
MLA_ROPE = 64
MLA_V = 128
MLA_QK = MLA_NOPE + MLA_ROPE
MLA_QK_PAD = 256
MLA_KC = 512
MLA_VT_ROWS = MLA_V + 16
MLA_AHEAD = 2
MLA_LOG2_SCALE = float(MLA_QK ** -0.5 * np.log2(np.e))
GQA_HEADS = 8
GQA_KV_HEADS = 2
GQA_GROUP = GQA_HEADS // GQA_KV_HEADS
GQA_HEAD_DIM = 64
GQA_WINDOW = 128
GQA_Q_W = GQA_HEADS * GQA_HEAD_DIM
GQA_KV_W = GQA_KV_HEADS * GQA_HEAD_DIM
BLOCK = 128
N_BRANCH = 3
D_FF = 2816
ROPE_THETA = 10000.0
EPS = 1e-6
NEG_INF = -1e30

LANES = 128
ONES_ROWS = 16
IN1_W = 1408
IN2_W = 1664
FF_CHUNK = 256
N_FF_CHUNKS = D_FF // FF_CHUNK
FF_AHEAD = 2
FF_BUFS = FF_AHEAD + 1
HALO = 16
SUBLANES = 8
VMEM_LIMIT = 56 * 1024 * 1024

_O_NAK = 0
_O_CQ = 512
_O_CKV = 896
_O_GK = 1152
_O_KR = 1280
_T_NAQ = 0
_T_NAV = 512
_T_GQ = 1024
_T_GV = 1536


def _rms(x, g):
    return x * lax.rsqrt(jnp.mean(x * x, axis=-1, keepdims=True) + EPS) * g


def _rope128(blk, c, s1, s2):
    return blk * c + pltpu.roll(blk, 32, 1) * s1 + pltpu.roll(blk, 96, 1) * s2


def _dot(a, b):
    return jnp.dot(a, b, preferred_element_type=F32)


def _dot_nt(a, b):
    return lax.dot_general(a, b, (((1,), (1,)), ((), ())), preferred_element_type=F32)


def _dot_tn(a, b):
    return lax.dot_general(a, b, (((0,), (0,)), ((), ())), preferred_element_type=F32)


def _ones_rows(n):
    return (lax.broadcasted_iota(jnp.int32, (ONES_ROWS, n), 0) == 0).astype(BF16)


def _params(n_axes):
    return pltpu.CompilerParams(dimension_semantics=("arbitrary",) * n_axes,
                                vmem_limit_bytes=VMEM_LIMIT)


def _const_spec(shape):
    n = len(shape)
    return pl.BlockSpec(shape, lambda *_: (0,) * n, pipeline_mode=pl.Buffered(1))


def _inproj_kernel(x_ref, g1_ref, w1_ref, gqa_ref, gkva_ref, wuqt_ref, wuk_ref, wuvt_ref, wgt_ref,
                   rope_ref, ropet_ref, naqt_ref, nak_ref, navt_ref, mqt_ref, mk_ref, mvt_ref,
                   gqt_ref, gk_ref, gvt_ref):
    tm = x_ref.shape[0]
    h = _rms(x_ref[...], g1_ref[...]).astype(BF16)
    z = _dot(h, w1_ref[...])
    gt = _dot_nt(wgt_ref[...], h)

    naqt_ref[...] = gt[_T_NAQ:_T_NAV, :].astype(BF16)
    for c in range(tm // LANES):
        navt_ref[c] = gt[_T_NAV:_T_GQ, c * LANES:(c + 1) * LANES].astype(BF16)
    nak_ref[...] = z[:, _O_NAK:_O_CQ].astype(BF16)

    mc, ms1, ms2 = rope_ref[:, 0:128], rope_ref[:, 128:256], rope_ref[:, 256:384]
    gc, gs1, gs2 = rope_ref[:, 384:512], rope_ref[:, 512:640], rope_ref[:, 640:768]

    cqn = _rms(z[:, _O_CQ:_O_CKV], gqa_ref[...]).astype(BF16)
    qt = _dot_nt(wuqt_ref[...], cqn) * MLA_LOG2_SCALE
    half = MLA_ROPE // 2
    ct, st = ropet_ref[0:half, :], ropet_ref[half:MLA_ROPE, :]
    for hd in range(MLA_HEADS):
        o = hd * MLA_QK_PAD
        r = o + MLA_NOPE
        mqt_ref[o:r, :] = qt[o:r, :].astype(BF16)
        x1, x2 = qt[r:r + half, :], qt[r + half:r + MLA_ROPE, :]
        mqt_ref[r:r + half, :] = (x1 * ct - x2 * st).astype(BF16)
        mqt_ref[r + half:r + MLA_ROPE, :] = (x1 * st + x2 * ct).astype(BF16)
        mqt_ref[r + MLA_ROPE:o + MLA_QK_PAD, :] = qt[r + MLA_ROPE:o + MLA_QK_PAD, :].astype(BF16)

    ckvn = _rms(z[:, _O_CKV:_O_GK], gkva_ref[...]).astype(BF16)
    kn = _dot(ckvn, wuk_ref[...])
    kpe = _rope128(z[:, _O_KR:_O_KR + LANES], mc, ms1, ms2).astype(BF16)
    for hd in range(MLA_HEADS):
        o = hd * MLA_QK_PAD
        mk_ref[:, o:o + LANES] = kn[:, hd * LANES:(hd + 1) * LANES].astype(BF16)
        mk_ref[:, o + LANES:o + 2 * LANES] = kpe
    vt = _dot_nt(wuvt_ref[...], ckvn)
    ones_rows = _ones_rows(MLA_KC)
    for hd in range(MLA_HEADS):
        for c in range(tm // MLA_KC):
            mvt_ref[hd, c, 0:MLA_V, :] = vt[hd * MLA_V:(hd + 1) * MLA_V,
                                            c * MLA_KC:(c + 1) * MLA_KC].astype(BF16)
            mvt_ref[hd, c, MLA_V:MLA_VT_ROWS, :] = ones_rows

    for hd in range(GQA_HEADS):
        r = _T_GQ + hd * GQA_HEAD_DIM
        x1, x2 = gt[r:r + half, :], gt[r + half:r + GQA_HEAD_DIM, :]
        o = hd * GQA_HEAD_DIM
        gqt_ref[o:o + half, :] = (x1 * ct - x2 * st).astype(BF16)
        gqt_ref[o + half:o + GQA_HEAD_DIM, :] = (x1 * st + x2 * ct).astype(BF16)
    gvt_ref[...] = gt[_T_GV:, :].astype(BF16)
    gk_ref[...] = _rope128(z[:, _O_GK:_O_KR], gc, gs1, gs2).astype(BF16)


def _inproj(x2d, g1, w1, g_qa, g_kva, wuqt, wuk, wuvt, wgt, rope, ropet, *, seq, tm):
    T = x2d.shape[0]
    tiles_per_seq = seq // tm
    cpt = tm // MLA_KC
    row = lambda w: pl.BlockSpec((tm, w), lambda i: (i, 0))
    col = lambda w: pl.BlockSpec((w, tm), lambda i: (0, i))
    qk_w = MLA_HEADS * MLA_QK_PAD
    out_specs = [col(NA_W), row(NA_W),
                 pl.BlockSpec((tm // LANES, NA_W, LANES), lambda i: (i, 0, 0)),
                 col(qk_w), row(qk_w),
                 pl.BlockSpec((MLA_HEADS, cpt, MLA_VT_ROWS, MLA_KC), lambda i: (0, i, 0, 0)),
                 col(GQA_Q_W), row(GQA_KV_W), col(GQA_KV_W)]
    out_shape = [jax.ShapeDtypeStruct((NA_W, T), BF16),
                 jax.ShapeDtypeStruct((T, NA_W), BF16),
                 jax.ShapeDtypeStruct((T // LANES, NA_W, LANES), BF16),
                 jax.ShapeDtypeStruct((qk_w, T), BF16),
                 jax.ShapeDtypeStruct((T, qk_w), BF16),
                 jax.ShapeDtypeStruct((MLA_HEADS, T // MLA_KC, MLA_VT_ROWS, MLA_KC), BF16),
                 jax.ShapeDtypeStruct((GQA_Q_W, T), BF16),
                 jax.ShapeDtypeStruct((T, GQA_KV_W), BF16),
                 jax.ShapeDtypeStruct((GQA_KV_W, T), BF16)]
    return pl.pallas_call(
        _inproj_kernel,
        grid=(T // tm,),
        in_specs=[row(D_MODEL), _const_spec((1, D_MODEL)), _const_spec((D_MODEL, IN1_W)),
                  _const_spec((1, MLA_Q_RANK)), _const_spec((1, MLA_KV_RANK)),
                  _const_spec((qk_w, MLA_Q_RANK)),
                  _const_spec((MLA_KV_RANK, MLA_HEADS * MLA_NOPE)),
                  _const_spec((MLA_HEADS * MLA_V, MLA_KV_RANK)),
                  _const_spec((IN2_W, D_MODEL)),
                  pl.BlockSpec((tm, 6 * LANES), lambda i: (i % tiles_per_seq, 0)),
                  pl.BlockSpec((MLA_ROPE, tm), lambda i: (0, i % tiles_per_seq))],
        out_specs=out_specs,
        out_shape=out_shape,
        compiler_params=_params(1),
        name="inproj",
    )(x2d, g1, w1, g_qa, g_kva, wuqt, wuk, wuvt, wgt, rope, ropet)


NA_ROWS_PER_STEP = 8
NA_KEY_ROWS = 10
NA_CASES = 5
NA_AHEAD = 3


def _na_kernel(qt_ref, k_ref, vt_ref, tb_ref, o_ref, *, rows):
    step = pl.program_id(1)
    n_keys = NA_KEY_ROWS * GRID_W
    ones_rows = _ones_rows(n_keys)
    zeros = jnp.zeros((NA_HEAD_DIM, LANES), BF16)
    pairs = []
    for rp in range(NA_ROWS_PER_STEP // 2):
        r = step * NA_ROWS_PER_STEP + 2 * rp
        case = jnp.where(r == 0, 0, jnp.where(r == 2, 1, jnp.where(
            r == rows - 4, 3, jnp.where(r == rows - 2, 4, 2))))
        u = jnp.clip(r - NA_WIN_R // 2, 0, rows - NA_KEY_ROWS)
        pairs.append((case, pl.multiple_of(u * GRID_W, LANES), u // 2))

    def scores(rp, hp):
        case, tok0, _ = pairs[rp]
        lo, mid, hi = hp * LANES, hp * LANES + NA_HEAD_DIM, (hp + 1) * LANES
        qcols = slice(rp * LANES, (rp + 1) * LANES)
        qa, qb = qt_ref[lo:mid, qcols], qt_ref[mid:hi, qcols]
        rhs = jnp.concatenate([jnp.concatenate([qa, zeros], axis=1),
                               jnp.concatenate([zeros, qb], axis=1)], axis=0)
        bias = jnp.concatenate([tb_ref[case, 2 * hp], tb_ref[case, 2 * hp + 1]], axis=1)
        return _dot(k_ref[pl.ds(tok0, n_keys), lo:hi], rhs) + bias

    def finish(rp, hp, t):
        _, _, c0 = pairs[rp]
        lo, mid, hi = hp * LANES, hp * LANES + NA_HEAD_DIM, (hp + 1) * LANES
        qcols = slice(rp * LANES, (rp + 1) * LANES)
        m = jnp.max(t, axis=0, keepdims=True)
        p = jnp.exp(t - m).astype(BF16)
        vt_win = jnp.concatenate([vt_ref[c0 + j, lo:hi, :] for j in range(n_keys // LANES)], axis=1)
        lhs = jnp.concatenate([vt_win, ones_rows], axis=0)
        acc = _dot(lhs, p)
        l = acc[LANES:LANES + 1, :]
        o_ref[lo:mid, qcols] = (acc[0:NA_HEAD_DIM, 0:LANES] / l[:, 0:LANES]).astype(BF16)
        o_ref[mid:hi, qcols] = (acc[NA_HEAD_DIM:LANES, LANES:] / l[:, LANES:]).astype(BF16)

    units = [(rp, hp) for rp in range(NA_ROWS_PER_STEP // 2) for hp in range(NA_HEADS // 2)]
    pending = [scores(*u) for u in units[:NA_AHEAD]]
    for idx, unit in enumerate(units):
        if idx + NA_AHEAD < len(units):
            pending.append(scores(*units[idx + NA_AHEAD]))
        finish(*unit, pending.pop(0))


def _na(naqt, nak, navt, tb, *, batch, seq):
    rows = seq // GRID_W
    tq = NA_ROWS_PER_STEP * GRID_W
    n_steps = seq // tq
    T = batch * seq
    k3 = nak.reshape(batch, seq, NA_W)
    vt4 = navt.reshape(batch, seq // LANES, NA_W, LANES)
    return pl.pallas_call(
        functools.partial(_na_kernel, rows=rows),
        grid=(batch, n_steps),
        in_specs=[pl.BlockSpec((NA_W, tq), lambda b, i: (0, b * n_steps + i)),
                  pl.BlockSpec((None, seq, NA_W), lambda b, i: (b, 0, 0), pipeline_mode=pl.Buffered(1)),
                  pl.BlockSpec((None, seq // LANES, NA_W, LANES), lambda b, i: (b, 0, 0, 0),
                               pipeline_mode=pl.Buffered(1)),
                  _const_spec(tb.shape)],
        out_specs=pl.BlockSpec((NA_W, tq), lambda b, i: (0, b * n_steps + i)),
        out_shape=jax.ShapeDtypeStruct((NA_W, T), BF16),
        compiler_params=_params(2),
        name="na",
    )(naqt, k3, vt4, tb)


def _na_bias_table(rpb, rows):
    kc = np.arange(GRID_W)[:, None]
    c = np.arange(GRID_W)[None, :]
    c_start = np.clip(c - NA_WIN_C // 2, 0, GRID_W - NA_WIN_C)
    valid_c = (kc >= c_start) & (kc < c_start + NA_WIN_C)
    side = GRID_W - NA_WIN_C
    padded = jnp.pad(rpb.astype(F32), ((0, 0), (0, 0), (side, side)))
    toe = jnp.stack([padded[:, :, GRID_W - 1 - cc:2 * GRID_W - 1 - cc] for cc in range(GRID_W)], axis=-1)
    toe = jnp.where(valid_c[None, None], toe, NEG_INF)
    neg = jnp.full((rpb.shape[0], GRID_W, GRID_W), NEG_INF, F32)
    tabs = []
    for r in (0, 2, 4, rows - 4, rows - 2):
        u = min(max(r - NA_WIN_R // 2, 0), rows - NA_KEY_ROWS)
        key_rows = []
        for i in range(NA_KEY_ROWS):
            blocks = []
            for rr in range(2):
                r_start = min(max(r + rr - NA_WIN_R // 2, 0), rows - NA_WIN_R)
                inside = r_start <= u + i < r_start + NA_WIN_R
                blocks.append(toe[:, u + i - (r + rr) + (NA_WIN_R - 1)] if inside else neg)
            key_rows.append(jnp.concatenate(blocks, axis=-1))
        tabs.append(jnp.concatenate(key_rows, axis=1))
    return jnp.stack(tabs)


def _mla_kernel(qt_ref, k_ref, vt_ref, o_ref, *, n_chunks):
    qt = qt_ref[...]
    tq = qt.shape[1]

    def scores(c):
        k0 = pl.multiple_of(c * MLA_KC, MLA_KC)
        return _dot(k_ref[pl.ds(k0, MLA_KC), :], qt)

    def update(c, m, acc, t):
        m_new = jnp.maximum(m, jnp.max(t, axis=0, keepdims=True))
        alpha = jnp.exp2(m - m_new)
        p = jnp.exp2(t - m_new).astype(BF16)
        return m_new, alpha * acc + _dot(vt_ref[c], p)

    m = jnp.full((1, tq), -jnp.inf, F32)
    acc = jnp.zeros((MLA_VT_ROWS, tq), F32)
    pending = [scores(c) for c in range(MLA_AHEAD)]
    for c in range(n_chunks):
        if c + MLA_AHEAD < n_chunks:
            pending.append(scores(c + MLA_AHEAD))
        m, acc = update(c, m, acc, pending.pop(0))
    o = acc[0:MLA_V, :] / acc[MLA_V:MLA_V + 1, :]
    o_ref[...] = o.T.astype(BF16)


def _mla(mqt, mk, mvt, *, batch, seq, tq):
    n_chunks = seq // MLA_KC
    nq = seq // tq
    k3 = mk.reshape(batch, seq, MLA_HEADS * MLA_QK_PAD)
    return pl.pallas_call(
        functools.partial(_mla_kernel, n_chunks=n_chunks),
        grid=(batch, MLA_HEADS, nq),
        in_specs=[pl.BlockSpec((MLA_QK_PAD, tq), lambda b, h, i: (h, b * nq + i)),
                  pl.BlockSpec((None, seq, MLA_QK_PAD), lambda b, h, i: (b, 0, h)),
                  pl.BlockSpec((None, n_chunks, MLA_VT_ROWS, MLA_KC), lambda b, h, i: (h, b, 0, 0))],
        out_specs=pl.BlockSpec((None, tq, MLA_V), lambda b, h, i: (b, i, h)),
        out_shape=jax.ShapeDtypeStruct((batch, seq, MLA_HEADS * MLA_V), BF16),
        compiler_params=_params(3),
        name="mla",
    )(mqt, k3, mvt)


GQA_BLOCKS_PER_STEP = 4
GQA_AHEAD = 1


def _gqa_kernel(qt_ref, kp_ref, kc_ref, kn_ref, vp_ref, vc_ref, vn_ref, b_ref, sk_ref, o_ref, *,
                n_steps):
    i = pl.program_id(1)
    nbs = GQA_BLOCKS_PER_STEP
    k_all = jnp.concatenate([kp_ref[...], kc_ref[...], kn_ref[...]], axis=0)
    vt_all = jnp.concatenate([vp_ref[...], vc_ref[...], vn_ref[...]], axis=1)
    ones_rows = _ones_rows(3 * BLOCK)
    zeros = jnp.zeros((GQA_HEAD_DIM, GQA_GROUP * BLOCK), BF16)

    def scores(j, kvh):
        if j == 0:
            bias = b_ref[jnp.where(i == 0, 0, 1)]
        elif j == nbs - 1:
            bias = b_ref[jnp.where(i == n_steps - 1, 2, 1)]
        else:
            bias = b_ref[1]
        bias = jnp.concatenate([bias] * GQA_GROUP, axis=1)
        heads = range(kvh * GQA_GROUP, (kvh + 1) * GQA_GROUP)
        qs = jnp.concatenate([qt_ref[hd * GQA_HEAD_DIM:(hd + 1) * GQA_HEAD_DIM,
                                     j * BLOCK:(j + 1) * BLOCK] for hd in heads], axis=1)
        rhs = jnp.concatenate([qs, zeros] if kvh == 0 else [zeros, qs], axis=0)
        return _dot(k_all[j * BLOCK:(j + 3) * BLOCK, :], rhs) + bias

    def finish(j, kvh, t):
        sink = sk_ref[kvh]
        m = jnp.maximum(jnp.max(t, axis=0, keepdims=True), sink)
        p = jnp.exp(t - m).astype(BF16)
        vt2 = vt_all[kvh * GQA_HEAD_DIM:(kvh + 1) * GQA_HEAD_DIM, j * BLOCK:(j + 3) * BLOCK]
        acc = _dot(jnp.concatenate([vt2, ones_rows], axis=0), p)
        l = acc[GQA_HEAD_DIM:GQA_HEAD_DIM + 1, :] + jnp.exp(sink - m)
        o = acc[0:GQA_HEAD_DIM, :] / l
        for g in range(GQA_GROUP):
            hd = kvh * GQA_GROUP + g
            o_ref[hd * GQA_HEAD_DIM:(hd + 1) * GQA_HEAD_DIM, j * BLOCK:(j + 1) * BLOCK] = (
                o[:, g * BLOCK:(g + 1) * BLOCK].astype(BF16))

    units = [(j, kvh) for j in range(nbs) for kvh in range(GQA_KV_HEADS)]
    pending = [scores(*u) for u in units[:GQA_AHEAD]]
    for idx, unit in enumerate(units):
        if idx + GQA_AHEAD < len(units):
            pending.append(scores(*units[idx + GQA_AHEAD]))
        finish(*unit, pending.pop(0))


def _gqa(gqt, gk, gvt, bias3, sink, *, batch, seq):
    nb = seq // BLOCK
    nbs = GQA_BLOCKS_PER_STEP
    n_steps = nb // nbs
    T = batch * seq
    k3 = gk.reshape(batch, seq, GQA_KV_W)
    prev = lambda i: jnp.maximum(i * nbs - 1, 0)
    nxt = lambda i: jnp.minimum((i + 1) * nbs, nb - 1)
    return pl.pallas_call(
        functools.partial(_gqa_kernel, n_steps=n_steps),
        grid=(batch, n_steps),
        in_specs=[pl.BlockSpec((GQA_Q_W, nbs * BLOCK), lambda b, i: (0, b * n_steps + i)),
                  pl.BlockSpec((None, BLOCK, GQA_KV_W), lambda b, i: (b, prev(i), 0)),
                  pl.BlockSpec((None, nbs * BLOCK, GQA_KV_W), lambda b, i: (b, i, 0)),
                  pl.BlockSpec((None, BLOCK, GQA_KV_W), lambda b, i: (b, nxt(i), 0)),
                  pl.BlockSpec((GQA_KV_W, BLOCK), lambda b, i: (0, b * nb + prev(i))),
                  pl.BlockSpec((GQA_KV_W, nbs * BLOCK), lambda b, i: (0, b * n_steps + i)),
                  pl.BlockSpec((GQA_KV_W, BLOCK), lambda b, i: (0, b * nb + nxt(i))),
                  _const_spec((3, 3 * BLOCK, BLOCK)),
                  _const_spec((GQA_KV_HEADS, 1, GQA_GROUP * BLOCK))],
        out_specs=pl.BlockSpec((GQA_Q_W, nbs * BLOCK), lambda b, i: (0, b * n_steps + i)),
        out_shape=jax.ShapeDtypeStruct((GQA_Q_W, T), BF16),
        compiler_params=_params(2),
        name="gqa",
    )(gqt, k3, k3, k3, gvt, gvt, gvt, bias3, sink)


def _gqa_mask_bias(nb):
    i = np.arange(BLOCK)[None, :]
    j = np.arange(3 * BLOCK)[:, None]
    band = np.abs(BLOCK + i - j) <= GQA_WINDOW
    first = band & (j >= BLOCK)
    last = band & (j < 2 * BLOCK)
    if nb == 1:
        first = last = first & last
    m = np.stack([first, band, last])
    return jnp.asarray(np.where(m, 0.0, NEG_INF), F32)


def _mixout_kernel(x_ref, ynat_ref, ymla_ref, ygqat_ref, g1_ref, wg_ref, bg_ref, wbr_ref, wo_ref, o_ref):
    x = x_ref[...]
    h = _rms(x, g1_ref[...]).astype(BF16)
    branches = (_dot_tn(ynat_ref[...], wbr_ref[0]), _dot(ymla_ref[...], wbr_ref[1]),
                _dot_tn(ygqat_ref[...], wbr_ref[2]))
    merged = None
    for i, br in enumerate(branches):
        cols = slice(i * D_MODEL, (i + 1) * D_MODEL)
        gate = jax.nn.sigmoid(_dot(h, wg_ref[:, cols]) + bg_ref[:, cols])
        merged = gate * br if merged is None else merged + gate * br
    o_ref[...] = x + _dot(merged.astype(BF16), wo_ref[...])


def _mixout(x2d, y_na_t, y_mla, y_gqa_t, g1, wg, bg, wbr, wo, *, tm):
    T = x2d.shape[0]
    row = lambda w: pl.BlockSpec((tm, w), lambda i: (i, 0))
    col = lambda w: pl.BlockSpec((w, tm), lambda i: (0, i))
    return pl.pallas_call(
        _mixout_kernel,
        grid=(T // tm,),
        in_specs=[row(D_MODEL), col(NA_W), row(MLA_HEADS * MLA_V), col(GQA_Q_W),
                  _const_spec((1, D_MODEL)), _const_spec((D_MODEL, N_BRANCH * D_MODEL)),
                  _const_spec((1, N_BRANCH * D_MODEL)), _const_spec((N_BRANCH, NA_W, D_MODEL)),
                  _const_spec((D_MODEL, D_MODEL))],
        out_specs=row(D_MODEL),
        out_shape=jax.ShapeDtypeStruct((T, D_MODEL), F32),
        compiler_params=_params(1),
        name="mixout",
    )(x2d, y_na_t, y_mla, y_gqa_t, g1, wg, bg, wbr, wo)


def _ffn_kernel(x_ref, xp_ref, xn_ref, g2_ref, wup_ref, cw_ref, cb_ref, wd_ref, fg_ref, o_ref,
                hx_ref, u_ref, acc_ref, *, tm, tiles_per_seq, final):
    j = pl.program_id(0) % tiles_per_seq
    nv = tm // SUBLANES
    g2 = g2_ref[...]
    kt = D_MODEL // LANES
    x = jnp.concatenate([
        jnp.concatenate([x_ref[pl.ds(q * kt + k, SUBLANES, stride=nv * kt), :] for k in range(kt)], axis=1)
        for q in range(nv)], axis=0)
    hx_ref[0:tm, :] = _rms(x, g2).astype(BF16)
    hp = jnp.where(j == 0, 0.0, _rms(xp_ref[SUBLANES - 1:SUBLANES, :], g2))
    hn = jnp.where(j == tiles_per_seq - 1, 0.0, _rms(xn_ref[0:1, :], g2))
    row = lax.broadcasted_iota(jnp.int32, (HALO, D_MODEL), 0)
    hx_ref[tm:tm + HALO, :] = jnp.where(row == 0, hp, jnp.where(row == 1, hn, 0.0)).astype(BF16)
    sub = lax.broadcasted_iota(jnp.int32, (SUBLANES, 2 * FF_CHUNK), 0)

    def up(c):
        u_ref[c % FF_BUFS] = _dot(hx_ref[...], wup_ref[c])

    def down(c):
        ub = u_ref.at[c % FF_BUFS]
        cw = cw_ref[c]
        cb = cb_ref[c]
        w0, w1, w2 = cw[0:1], cw[1:2], cw[2:3]
        halo = ub[tm:tm + SUBLANES, :]
        u_first, u_last = ub[0:SUBLANES, :], ub[tm - SUBLANES:tm, :]
        before_first = jnp.where(sub == 0, halo[0:1], pltpu.roll(u_last, 1, 0))
        after_last = jnp.where(sub == SUBLANES - 1, halo[1:2], pltpu.roll(u_first, SUBLANES - 1, 0))
        y = jnp.concatenate([
            w0 * before_first + w1 * u_first + w2 * ub[SUBLANES:2 * SUBLANES, :] + cb,
            w0 * ub[0:tm - 2 * SUBLANES, :] + w1 * ub[SUBLANES:tm - SUBLANES, :]
            + w2 * ub[2 * SUBLANES:tm, :] + cb,
            w0 * ub[tm - 2 * SUBLANES:tm - SUBLANES, :] + w1 * u_last + w2 * after_last + cb], axis=0)
        act = jax.nn.gelu(y[:, FF_CHUNK:]) * y[:, :FF_CHUNK]
        acc_ref[...] += _dot(act.astype(BF16), wd_ref[c])

    acc_ref[...] = jnp.zeros(acc_ref.shape, F32)
    for c in range(FF_AHEAD):
        up(c)
    for c in range(N_FF_CHUNKS):
        if c + FF_AHEAD < N_FF_CHUNKS:
            up(c + FF_AHEAD)
        down(c)
    out = x + acc_ref[...]
    if final:
        out = _rms(out, fg_ref[...])
    for q in range(nv):
        for k in range(kt):
            o_ref[pl.ds(q * kt + k, SUBLANES, stride=nv * kt), :] = (
                out[q * SUBLANES:(q + 1) * SUBLANES, k * LANES:(k + 1) * LANES])


def _ffn(x2d, g2, wup, cw, cb, wd, fg, *, seq, tm, final):
    T = x2d.shape[0]
    tiles_per_seq = seq // tm
    hb = tm // SUBLANES
    n_hb = T // SUBLANES
    kt = D_MODEL // LANES
    return pl.pallas_call(
        functools.partial(_ffn_kernel, tm=tm, tiles_per_seq=tiles_per_seq, final=final),
        grid=(T // tm,),
        in_specs=[pl.BlockSpec((tm * kt, LANES), lambda i: (i, 0)),
                  pl.BlockSpec((SUBLANES, D_MODEL), lambda i: (jnp.maximum(i * hb - 1, 0), 0)),
                  pl.BlockSpec((SUBLANES, D_MODEL), lambda i: (jnp.minimum((i + 1) * hb, n_hb - 1), 0)),
                  _const_spec((1, D_MODEL)),
                  _const_spec((N_FF_CHUNKS, D_MODEL, 2 * FF_CHUNK)),
                  _const_spec((N_FF_CHUNKS, 3, 2 * FF_CHUNK)),
                  _const_spec((N_FF_CHUNKS, 1, 2 * FF_CHUNK)),
                  _const_spec((N_FF_CHUNKS, FF_CHUNK, D_MODEL)),
                  _const_spec((1, D_MODEL))],
        out_specs=pl.BlockSpec((tm * kt, LANES), lambda i: (i, 0)),
        out_shape=jax.ShapeDtypeStruct((T * kt, LANES), F32),
        scratch_shapes=[pltpu.VMEM((tm + HALO, D_MODEL), BF16),
                        pltpu.VMEM((FF_BUFS, tm + HALO, 2 * FF_CHUNK), F32),
                        pltpu.VMEM((tm, D_MODEL), F32)],
        compiler_params=_params(1),
        name="ffn",
    )(x2d.reshape(T * kt, LANES), x2d, x2d, g2, wup, cw, cb, wd, fg).reshape(T, D_MODEL)


def _rope_tables(seq):
    def cs(dim):
        inv = 1.0 / (ROPE_THETA ** (jnp.arange(0, dim, 2, dtype=F32) / dim))
        ang = jnp.arange(seq, dtype=F32)[:, None] * inv[None, :]
        return jnp.cos(ang), jnp.sin(ang)

    def head(c, s):
        z = jnp.zeros_like(s)
        return (jnp.concatenate([c, c], -1), jnp.concatenate([z, s], -1), jnp.concatenate([-s, z], -1))

    m_cos, m_sin = cs(MLA_ROPE)
    mc, ms1, ms2 = head(m_cos, m_sin)
    pad = jnp.zeros_like(mc)
    gc, gs1, gs2 = head(*cs(GQA_HEAD_DIM))
    two = lambda a: jnp.concatenate([a, a], -1)
    table = jnp.concatenate([mc, pad, ms1, pad, ms2, pad, two(gc), two(gs1), two(gs2)], axis=-1)
    table_t = jnp.concatenate([m_cos, m_sin], axis=-1).T
    return table, table_t


def _prep_layer(w_in, b_gate, mla_w_uq, mla_w_ukv, w_br_na, w_br_mla, w_br_gqa, w_out,
                w_up, conv_w, conv_b, w_down):
    o_nak = NA_W
    o_nav = 2 * NA_W
    o_cq = 3 * NA_W
    o_ckv = o_cq + MLA_Q_RANK
    o_kr = o_ckv + MLA_KV_RANK
    o_gq = o_kr + MLA_ROPE
    o_gkv = o_gq + GQA_Q_W
    o_gv = o_gkv + GQA_KV_W
    o_gate = o_gv + GQA_KV_W
    na_scale = NA_HEAD_DIM ** -0.5
    gqa_scale = GQA_HEAD_DIM ** -0.5
    w1 = jnp.concatenate([
        w_in[:, o_nak:o_nav], w_in[:, o_cq:o_ckv], w_in[:, o_ckv:o_kr], w_in[:, o_gkv:o_gv],
        w_in[:, o_kr:o_gq], jnp.zeros((D_MODEL, LANES - MLA_ROPE), w_in.dtype)], axis=1).astype(BF16)
    wgt = jnp.concatenate([w_in[:, 0:o_nak] * na_scale, w_in[:, o_nav:o_cq],
                           w_in[:, o_gq:o_gkv] * gqa_scale, w_in[:, o_gv:o_gate]], axis=1).T.astype(BF16)
    wg = w_in[:, o_gate:].astype(BF16)
    wuqt = jnp.pad(mla_w_uq.reshape(MLA_Q_RANK, MLA_HEADS, MLA_QK),
                   ((0, 0), (0, 0), (0, MLA_QK_PAD - MLA_QK))).reshape(MLA_Q_RANK, -1).T.astype(BF16)
    ukv = mla_w_ukv.reshape(MLA_KV_RANK, MLA_HEADS, MLA_NOPE + MLA_V)
    wuk = ukv[:, :, :MLA_NOPE].reshape(MLA_KV_RANK, -1).astype(BF16)
    wuvt = ukv[:, :, MLA_NOPE:].reshape(MLA_KV_RANK, -1).T.astype(BF16)
    wbr = jnp.stack([w_br_na, w_br_mla, w_br_gqa]).astype(BF16)
    chunks = lambda a: a.reshape(a.shape[0], 2, N_FF_CHUNKS, FF_CHUNK)
    wup = chunks(w_up).transpose(2, 0, 1, 3).reshape(N_FF_CHUNKS, D_MODEL, 2 * FF_CHUNK).astype(BF16)
    cw = chunks(conv_w).transpose(2, 0, 1, 3).reshape(N_FF_CHUNKS, 3, 2 * FF_CHUNK)
    cb = chunks(conv_b[None]).transpose(2, 0, 1, 3).reshape(N_FF_CHUNKS, 1, 2 * FF_CHUNK)
    wd = w_down.reshape(N_FF_CHUNKS, FF_CHUNK, D_MODEL).astype(BF16)
    return dict(w1=w1, wgt=wgt, wg=wg, bg=b_gate[None], wuqt=wuqt, wuk=wuk, wuvt=wuvt, wbr=wbr,
                wo=w_out.astype(BF16), wup=wup, cw=cw, cb=cb, wd=wd)


def _tiles(seq):
    tm = min(512, seq)
    tq = min(1024, seq)
    return tm, tq


def kernel(x, norm1_g, w_in, b_gate, na_rpb, mla_qa_g, mla_kva_g, mla_w_uq, mla_w_ukv, gqa_sink, w_br_na, w_br_mla, w_br_gqa, w_out, norm2_g, w_up, conv_w, conv_b, w_down, final_g):
    batch, seq, _ = x.shape
    depth = w_in.shape[0]
    rows = seq // GRID_W
    assert seq % (NA_ROWS_PER_STEP * GRID_W) == 0 and rows >= 12 and rows % 2 == 0
    assert seq % (GQA_BLOCKS_PER_STEP * BLOCK) == 0 and MLA_ROPE == GQA_HEAD_DIM
    tm, tq = _tiles(seq)
    assert tm % MLA_KC == 0 and seq % tq == 0
    rope, rope_t = _rope_tables(seq)
    gqa_bias = _gqa_mask_bias(seq // BLOCK)
    xf = x.reshape(batch * seq, D_MODEL)
    for l in range(depth):
        p = _prep_layer(w_in[l], b_gate[l], mla_w_uq[l], mla_w_ukv[l], w_br_na[l], w_br_mla[l],
                        w_br_gqa[l], w_out[l], w_up[l], conv_w[l], conv_b[l], w_down[l])
        g1 = norm1_g[l][None]
        naqt, nak, navt, mqt, mk, mvt, gqt, gk, gvt = _inproj(
            xf, g1, p["w1"], mla_qa_g[l][None], mla_kva_g[l][None], p["wuqt"], p["wuk"], p["wuvt"],
            p["wgt"], rope, rope_t, seq=seq, tm=tm)
        y_na_t = _na(naqt, nak, navt, _na_bias_table(na_rpb[l], rows), batch=batch, seq=seq)
        y_mla = _mla(mqt, mk, mvt, batch=batch, seq=seq, tq=tq).reshape(batch * seq, -1)
        sink = jnp.repeat(gqa_sink[l].astype(F32).reshape(GQA_KV_HEADS, 1, GQA_GROUP), BLOCK, axis=-1)
        y_gqa_t = _gqa(gqt, gk, gvt, gqa_bias, sink, batch=batch, seq=seq)
        xf = _mixout(xf, y_na_t, y_mla, y_gqa_t, g1, p["wg"], p["bg"], p["wbr"], p["wo"], tm=tm)
        xf = _ffn(xf, norm2_g[l][None], p["wup"], p["cw"], p["cb"], p["wd"], final_g[None],
                  seq=seq, tm=tm, final=(l == depth - 1))
    return xf.reshape(batch, seq, D_MODEL)
```

```python
import functools

import jax
import jax.numpy as jnp
import numpy as np
from jax import lax
from jax.experimental import pallas as pl
from jax.experimental.pallas import tpu as pltpu

F32 = jnp.float32
BF16 = jnp.bfloat16

D_MODEL = 1024
GRID_W = 64
NA_HEADS = 8
NA_HEAD_DIM = 64
NA_WIN_R = 8
NA_WIN_C = 16
NA_W = NA_HEADS * NA_HEAD_DIM
MLA_HEADS = 4
MLA_Q_RANK = 384
MLA_KV_RANK = 256
MLA_NOPE = 128
MLA_ROPE = 64
MLA_V = 128
MLA_QK = MLA_NOPE + MLA_ROPE
MLA_QK_PAD = 256
MLA_KC = 512
MLA_VT_ROWS = MLA_V + 16
MLA_AHEAD = 1
MLA_LOG2_SCALE = float(MLA_QK ** -0.5 * np.log2(np.e))
GQA_HEADS = 8
GQA_KV_HEADS = 2
GQA_GROUP = GQA_HEADS // GQA_KV_HEADS
GQA_HEAD_DIM = 64
GQA_WINDOW = 128
GQA_Q_W = GQA_HEADS * GQA_HEAD_DIM
GQA_KV_W = GQA_KV_HEADS * GQA_HEAD_DIM
BLOCK = 128
N_BRANCH = 3
D_FF = 2816
ROPE_THETA = 10000.0
EPS = 1e-6
NEG_INF = -1e30

LANES = 128
ONES_ROWS = 16
IN1_W = 1408
IN2_W = 1664
FF_CHUNK = 256
N_FF_CHUNKS = D_FF // FF_CHUNK
HALO = 16
VMEM_LIMIT = 56 * 1024 * 1024

_O_NAK = 0
_O_CQ = 512
_O_CKV = 896
_O_GK = 1152
_O_KR = 1280
_T_NAQ = 0
_T_NAV = 512
_T_GQ = 1024
_T_GV = 1536


def _rms(x, g):
    return x * lax.rsqrt(jnp.mean(x * x, axis=-1, keepdims=True) + EPS) * g


def _rope128(blk, c, s1, s2):
    return blk * c + pltpu.roll(blk, 32, 1) * s1 + pltpu.roll(blk, 96, 1) * s2


def _dot(a, b):
    return jnp.dot(a, b, preferred_element_type=F32)


def _dot_nt(a, b):
    return lax.dot_general(a, b, (((1,), (1,)), ((), ())), preferred_element_type=F32)


def _dot_tn(a, b):
    return lax.dot_general(a, b, (((0,), (0,)), ((), ())), preferred_element_type=F32)


def _ones_rows(n):
    return (lax.broadcasted_iota(jnp.int32, (ONES_ROWS, n), 0) == 0).astype(BF16)


def _params(n_axes):
    return pltpu.CompilerParams(dimension_semantics=("arbitrary",) * n_axes,
                                vmem_limit_bytes=VMEM_LIMIT)


def _const_spec(shape):
    n = len(shape)
    return pl.BlockSpec(shape, lambda *_: (0,) * n, pipeline_mode=pl.Buffered(1))


def _inproj_kernel(x_ref, g1_ref, w1_ref, gqa_ref, gkva_ref, wuqt_ref, wuk_ref, wuvt_ref, wgt_ref,
                   rope_ref, ropet_ref, naqt_ref, nak_ref, navt_ref, mqt_ref, mk_ref, mvt_ref,
                   gqt_ref, gk_ref, gvt_ref):
    tm = x_ref.shape[0]
    h = _rms(x_ref[...], g1_ref[...]).astype(BF16)
    z = _dot(h, w1_ref[...])
    gt = _dot_nt(wgt_ref[...], h)

    naqt_ref[...] = gt[_T_NAQ:_T_NAV, :].astype(BF16)
    for c in range(tm // LANES):
        navt_ref[c] = gt[_T_NAV:_T_GQ, c * LANES:(c + 1) * LANES].astype(BF16)
    nak_ref[...] = z[:, _O_NAK:_O_CQ].astype(BF16)

    mc, ms1, ms2 = rope_ref[:, 0:128], rope_ref[:, 128:256], rope_ref[:, 256:384]
    gc, gs1, gs2 = rope_ref[:, 384:512], rope_ref[:, 512:640], rope_ref[:, 640:768]

    cqn = _rms(z[:, _O_CQ:_O_CKV], gqa_ref[...]).astype(BF16)
    qt = _dot_nt(wuqt_ref[...], cqn) * MLA_LOG2_SCALE
    half = MLA_ROPE // 2
    ct, st = ropet_ref[0:half, :], ropet_ref[half:MLA_ROPE, :]
    for hd in range(MLA_HEADS):
        o = hd * MLA_QK_PAD
        r = o + MLA_NOPE
        mqt_ref[o:r, :] = qt[o:r, :].astype(BF16)
        x1, x2 = qt[r:r + half, :], qt[r + half:r + MLA_ROPE, :]
        mqt_ref[r:r + half, :] = (x1 * ct - x2 * st).astype(BF16)
        mqt_ref[r + half:r + MLA_ROPE, :] = (x1 * st + x2 * ct).astype(BF16)
        mqt_ref[r + MLA_ROPE:o + MLA_QK_PAD, :] = qt[r + MLA_ROPE:o + MLA_QK_PAD, :].astype(BF16)

    ckvn = _rms(z[:, _O_CKV:_O_GK], gkva_ref[...]).astype(BF16)
    kn = _dot(ckvn, wuk_ref[...])
    kpe = _rope128(z[:, _O_KR:_O_KR + LANES], mc, ms1, ms2).astype(BF16)
    for hd in range(MLA_HEADS):
        o = hd * MLA_QK_PAD
        mk_ref[:, o:o + LANES] = kn[:, hd * LANES:(hd + 1) * LANES].astype(BF16)
        mk_ref[:, o + LANES:o + 2 * LANES] = kpe
    vt = _dot_nt(wuvt_ref[...], ckvn)
    ones_rows = _ones_rows(MLA_KC)
    for hd in range(MLA_HEADS):
        for c in range(tm // MLA_KC):
            mvt_ref[hd, c, 0:MLA_V, :] = vt[hd * MLA_V:(hd + 1) * MLA_V,
                                            c * MLA_KC:(c + 1) * MLA_KC].astype(BF16)
            mvt_ref[hd, c, MLA_V:MLA_VT_ROWS, :] = ones_rows

    for hd in range(GQA_HEADS):
        r = _T_GQ + hd * GQA_HEAD_DIM
        x1, x2 = gt[r:r + half, :], gt[r + half:r + GQA_HEAD_DIM, :]
        o = hd * GQA_HEAD_DIM
        gqt_ref[o:o + half, :] = (x1 * ct - x2 * st).astype(BF16)
        gqt_ref[o + half:o + GQA_HEAD_DIM, :] = (x1 * st + x2 * ct).astype(BF16)
    gvt_ref[...] = gt[_T_GV:, :].astype(BF16)
    gk_ref[...] = _rope128(z[:, _O_GK:_O_KR], gc, gs1, gs2).astype(BF16)


def _inproj(x2d, g1, w1, g_qa, g_kva, wuqt, wuk, wuvt, wgt, rope, ropet, *, seq, tm):
    T = x2d.shape[0]
    tiles_per_seq = seq // tm
    cpt = tm // MLA_KC
    row = lambda w: pl.BlockSpec((tm, w), lambda i: (i, 0))
    col = lambda w: pl.BlockSpec((w, tm), lambda i: (0, i))
    qk_w = MLA_HEADS * MLA_QK_PAD
    out_specs = [col(NA_W), row(NA_W),
                 pl.BlockSpec((tm // LANES, NA_W, LANES), lambda i: (i, 0, 0)),
                 col(qk_w), row(qk_w),
                 pl.BlockSpec((MLA_HEADS, cpt, MLA_VT_ROWS, MLA_KC), lambda i: (0, i, 0, 0)),
                 col(GQA_Q_W), row(GQA_KV_W), col(GQA_KV_W)]
    out_shape = [jax.ShapeDtypeStruct((NA_W, T), BF16),
                 jax.ShapeDtypeStruct((T, NA_W), BF16),
                 jax.ShapeDtypeStruct((T // LANES, NA_W, LANES), BF16),
                 jax.ShapeDtypeStruct((qk_w, T), BF16),
                 jax.ShapeDtypeStruct((T, qk_w), BF16),
                 jax.ShapeDtypeStruct((MLA_HEADS, T // MLA_KC, MLA_VT_ROWS, MLA_KC), BF16),
                 jax.ShapeDtypeStruct((GQA_Q_W, T), BF16),
                 jax.ShapeDtypeStruct((T, GQA_KV_W), BF16),
                 jax.ShapeDtypeStruct((GQA_KV_W, T), BF16)]
    return pl.pallas_call(
        _inproj_kernel,
        grid=(T // tm,),
        in_specs=[row(D_MODEL), _const_spec((1, D_MODEL)), _const_spec((D_MODEL, IN1_W)),
                  _const_spec((1, MLA_Q_RANK)), _const_spec((1, MLA_KV_RANK)),
                  _const_spec((qk_w, MLA_Q_RANK)),
                  _const_spec((MLA_KV_RANK, MLA_HEADS * MLA_NOPE)),
                  _const_spec((MLA_HEADS * MLA_V, MLA_KV_RANK)),
                  _const_spec((IN2_W, D_MODEL)),
                  pl.BlockSpec((tm, 6 * LANES), lambda i: (i % tiles_per_seq, 0)),
                  pl.BlockSpec((MLA_ROPE, tm), lambda i: (0, i % tiles_per_seq))],
        out_specs=out_specs,
        out_shape=out_shape,
        compiler_params=_params(1),
        name="inproj",
    )(x2d, g1, w1, g_qa, g_kva, wuqt, wuk, wuvt, wgt, rope, ropet)


NA_ROWS_PER_STEP = 16
NA_KEY_ROWS = 10
NA_CASES = 5
NA_AHEAD = 3


def _na_kernel(qt_ref, k_ref, vt_ref, tb_ref, o_ref, *, rows):
    step = pl.program_id(1)
    n_keys = NA_KEY_ROWS * GRID_W
    ones_rows = _ones_rows(n_keys)
    zeros = jnp.zeros((NA_HEAD_DIM, LANES), BF16)
    pairs = []
    for rp in range(NA_ROWS_PER_STEP // 2):
        r = step * NA_ROWS_PER_STEP + 2 * rp
        case = jnp.where(r == 0, 0, jnp.where(r == 2, 1, jnp.where(
            r == rows - 4, 3, jnp.where(r == rows - 2, 4, 2))))
        u = jnp.clip(r - NA_WIN_R // 2, 0, rows - NA_KEY_ROWS)
        pairs.append((case, pl.multiple_of(u * GRID_W, LANES), u // 2))

    def scores(rp, hp):
        case, tok0, _ = pairs[rp]
        lo, mid, hi = hp * LANES, hp * LANES + NA_HEAD_DIM, (hp + 1) * LANES
        qcols = slice(rp * LANES, (rp + 1) * LANES)
        qa, qb = qt_ref[lo:mid, qcols], qt_ref[mid:hi, qcols]
        rhs = jnp.concatenate([jnp.concatenate([qa, zeros], axis=1),
                               jnp.concatenate([zeros, qb], axis=1)], axis=0)
        bias = jnp.concatenate([tb_ref[case, 2 * hp], tb_ref[case, 2 * hp + 1]], axis=1)
        return _dot(k_ref[pl.ds(tok0, n_keys), lo:hi], rhs) + bias

    def finish(rp, hp, t):
        _, _, c0 = pairs[rp]
        lo, mid, hi = hp * LANES, hp * LANES + NA_HEAD_DIM, (hp + 1) * LANES
        qcols = slice(rp * LANES, (rp + 1) * LANES)
        m = jnp.max(t, axis=0, keepdims=True)
        p = jnp.exp(t - m).astype(BF16)
        vt_win = jnp.concatenate([vt_ref[c0 + j, lo:hi, :] for j in range(n_keys // LANES)], axis=1)
        lhs = jnp.concatenate([vt_win, ones_rows], axis=0)
        acc = _dot(lhs, p)
        l = acc[LANES:LANES + 1, :]
        o_ref[lo:mid, qcols] = (acc[0:NA_HEAD_DIM, 0:LANES] / l[:, 0:LANES]).astype(BF16)
        o_ref[mid:hi, qcols] = (acc[NA_HEAD_DIM:LANES, LANES:] / l[:, LANES:]).astype(BF16)

    units = [(rp, hp) for rp in range(NA_ROWS_PER_STEP // 2) for hp in range(NA_HEADS // 2)]
    pending = [scores(*u) for u in units[:NA_AHEAD]]
    for idx, unit in enumerate(units):
        if idx + NA_AHEAD < len(units):
            pending.append(scores(*units[idx + NA_AHEAD]))
        finish(*unit, pending.pop(0))


def _na(naqt, nak, navt, tb, *, batch, seq):
    rows = seq // GRID_W
    tq = NA_ROWS_PER_STEP * GRID_W
    n_steps = seq // tq
    T = batch * seq
    k3 = nak.reshape(batch, seq, NA_W)
    vt4 = navt.reshape(batch, seq // LANES, NA_W, LANES)
    return pl.pallas_call(
        functools.partial(_na_kernel, rows=rows),
        grid=(batch, n_steps),
        in_specs=[pl.BlockSpec((NA_W, tq), lambda b, i: (0, b * n_steps + i)),
                  pl.BlockSpec((None, seq, NA_W), lambda b, i: (b, 0, 0), pipeline_mode=pl.Buffered(1)),
                  pl.BlockSpec((None, seq // LANES, NA_W, LANES), lambda b, i: (b, 0, 0, 0),
                               pipeline_mode=pl.Buffered(1)),
                  _const_spec(tb.shape)],
        out_specs=pl.BlockSpec((NA_W, tq), lambda b, i: (0, b * n_steps + i)),
        out_shape=jax.ShapeDtypeStruct((NA_W, T), BF16),
        compiler_params=_params(2),
        name="na",
    )(naqt, k3, vt4, tb)


def _na_bias_table(rpb, rows):
    kc = np.arange(GRID_W)[:, None]
    c = np.arange(GRID_W)[None, :]
    c_start = np.clip(c - NA_WIN_C // 2, 0, GRID_W - NA_WIN_C)
    valid_c = (kc >= c_start) & (kc < c_start + NA_WIN_C)
    side = GRID_W - NA_WIN_C
    padded = jnp.pad(rpb.astype(F32), ((0, 0), (0, 0), (side, side)))
    toe = jnp.stack([padded[:, :, GRID_W - 1 - cc:2 * GRID_W - 1 - cc] for cc in range(GRID_W)], axis=-1)
    toe = jnp.where(valid_c[None, None], toe, NEG_INF)
    neg = jnp.full((rpb.shape[0], GRID_W, GRID_W), NEG_INF, F32)
    tabs = []
    for r in (0, 2, 4, rows - 4, rows - 2):
        u = min(max(r - NA_WIN_R // 2, 0), rows - NA_KEY_ROWS)
        key_rows = []
        for i in range(NA_KEY_ROWS):
            blocks = []
            for rr in range(2):
                r_start = min(max(r + rr - NA_WIN_R // 2, 0), rows - NA_WIN_R)
                inside = r_start <= u + i < r_start + NA_WIN_R
                blocks.append(toe[:, u + i - (r + rr) + (NA_WIN_R - 1)] if inside else neg)
            key_rows.append(jnp.concatenate(blocks, axis=-1))
        tabs.append(jnp.concatenate(key_rows, axis=1))
    return jnp.stack(tabs)


def _mla_kernel(qt_ref, k_ref, vt_ref, o_ref, *, n_chunks):
    qt = qt_ref[...]
    tq = qt.shape[1]

    def scores(c):
        k0 = pl.multiple_of(c * MLA_KC, MLA_KC)
        return _dot(k_ref[pl.ds(k0, MLA_KC), :], qt)

    def update(c, m, acc, t):
        m_new = jnp.maximum(m, jnp.max(t, axis=0, keepdims=True))
        alpha = jnp.exp2(m - m_new)
        p = jnp.exp2(t - m_new).astype(BF16)
        return m_new, alpha * acc + _dot(vt_ref[c], p)

    m = jnp.full((1, tq), -jnp.inf, F32)
    acc = jnp.zeros((MLA_VT_ROWS, tq), F32)
    pending = [scores(c) for c in range(MLA_AHEAD)]
    for c in range(n_chunks):
        if c + MLA_AHEAD < n_chunks:
            pending.append(scores(c + MLA_AHEAD))
        m, acc = update(c, m, acc, pending.pop(0))
    o = acc[0:MLA_V, :] / acc[MLA_V:MLA_V + 1, :]
    o_ref[...] = o.T.astype(BF16)


def _mla(mqt, mk, mvt, *, batch, seq, tq):
    n_chunks = seq // MLA_KC
    nq = seq // tq
    k3 = mk.reshape(batch, seq, MLA_HEADS * MLA_QK_PAD)
    return pl.pallas_call(
        functools.partial(_mla_kernel, n_chunks=n_chunks),
        grid=(batch, MLA_HEADS, nq),
        in_specs=[pl.BlockSpec((MLA_QK_PAD, tq), lambda b, h, i: (h, b * nq + i)),
                  pl.BlockSpec((None, seq, MLA_QK_PAD), lambda b, h, i: (b, 0, h)),
                  pl.BlockSpec((None, n_chunks, MLA_VT_ROWS, MLA_KC), lambda b, h, i: (h, b, 0, 0))],
        out_specs=pl.BlockSpec((None, tq, MLA_V), lambda b, h, i: (b, i, h)),
        out_shape=jax.ShapeDtypeStruct((batch, seq, MLA_HEADS * MLA_V), BF16),
        compiler_params=_params(3),
        name="mla",
    )(mqt, k3, mvt)


GQA_BLOCKS_PER_STEP = 4
GQA_AHEAD = 1


def _gqa_kernel(qt_ref, kp_ref, kc_ref, kn_ref, vp_ref, vc_ref, vn_ref, b_ref, sk_ref, o_ref, *,
                n_steps):
    i = pl.program_id(1)
    nbs = GQA_BLOCKS_PER_STEP
    k_all = jnp.concatenate([kp_ref[...], kc_ref[...], kn_ref[...]], axis=0)
    vt_all = jnp.concatenate([vp_ref[...], vc_ref[...], vn_ref[...]], axis=1)
    ones_rows = _ones_rows(3 * BLOCK)
    zeros = jnp.zeros((GQA_HEAD_DIM, GQA_GROUP * BLOCK), BF16)

    def scores(j, kvh):
        if j == 0:
            bias = b_ref[jnp.where(i == 0, 0, 1)]
        elif j == nbs - 1:
            bias = b_ref[jnp.where(i == n_steps - 1, 2, 1)]
        else:
            bias = b_ref[1]
        bias = jnp.concatenate([bias] * GQA_GROUP, axis=1)
        heads = range(kvh * GQA_GROUP, (kvh + 1) * GQA_GROUP)
        qs = jnp.concatenate([qt_ref[hd * GQA_HEAD_DIM:(hd + 1) * GQA_HEAD_DIM,
                                     j * BLOCK:(j + 1) * BLOCK] for hd in heads], axis=1)
        rhs = jnp.concatenate([qs, zeros] if kvh == 0 else [zeros, qs], axis=0)
        return _dot(k_all[j * BLOCK:(j + 3) * BLOCK, :], rhs) + bias

    def finish(j, kvh, t):
        sink = sk_ref[kvh]
        m = jnp.maximum(jnp.max(t, axis=0, keepdims=True), sink)
        p = jnp.exp(t - m).astype(BF16)
        vt2 = vt_all[kvh * GQA_HEAD_DIM:(kvh + 1) * GQA_HEAD_DIM, j * BLOCK:(j + 3) * BLOCK]
        acc = _dot(jnp.concatenate([vt2, ones_rows], axis=0), p)
        l = acc[GQA_HEAD_DIM:GQA_HEAD_DIM + 1, :] + jnp.exp(sink - m)
        o = acc[0:GQA_HEAD_DIM, :] / l
        for g in range(GQA_GROUP):
            hd = kvh * GQA_GROUP + g
            o_ref[hd * GQA_HEAD_DIM:(hd + 1) * GQA_HEAD_DIM, j * BLOCK:(j + 1) * BLOCK] = (
                o[:, g * BLOCK:(g + 1) * BLOCK].astype(BF16))

    units = [(j, kvh) for j in range(nbs) for kvh in range(GQA_KV_HEADS)]
    pending = [scores(*u) for u in units[:GQA_AHEAD]]
    for idx, unit in enumerate(units):
        if idx + GQA_AHEAD < len(units):
            pending.append(scores(*units[idx + GQA_AHEAD]))
        finish(*unit, pending.pop(0))


def _gqa(gqt, gk, gvt, bias3, sink, *, batch, seq):
    nb = seq // BLOCK
    nbs = GQA_BLOCKS_PER_STEP
    n_steps = nb // nbs
    T = batch * seq
    k3 = gk.reshape(batch, seq, GQA_KV_W)
    prev = lambda i: jnp.maximum(i * nbs - 1, 0)
    nxt = lambda i: jnp.minimum((i + 1) * nbs, nb - 1)
    return pl.pallas_call(
        functools.partial(_gqa_kernel, n_steps=n_steps),
        grid=(batch, n_steps),
        in_specs=[pl.BlockSpec((GQA_Q_W, nbs * BLOCK), lambda b, i: (0, b * n_steps + i)),
                  pl.BlockSpec((None, BLOCK, GQA_KV_W), lambda b, i: (b, prev(i), 0)),
                  pl.BlockSpec((None, nbs * BLOCK, GQA_KV_W), lambda b, i: (b, i, 0)),
                  pl.BlockSpec((None, BLOCK, GQA_KV_W), lambda b, i: (b, nxt(i), 0)),
                  pl.BlockSpec((GQA_KV_W, BLOCK), lambda b, i: (0, b * nb + prev(i))),
                  pl.BlockSpec((GQA_KV_W, nbs * BLOCK), lambda b, i: (0, b * n_steps + i)),
                  pl.BlockSpec((GQA_KV_W, BLOCK), lambda b, i: (0, b * nb + nxt(i))),
                  _const_spec((3, 3 * BLOCK, BLOCK)),
                  _const_spec((GQA_KV_HEADS, 1, GQA_GROUP * BLOCK))],
        out_specs=pl.BlockSpec((GQA_Q_W, nbs * BLOCK), lambda b, i: (0, b * n_steps + i)),
        out_shape=jax.ShapeDtypeStruct((GQA_Q_W, T), BF16),
        compiler_params=_params(2),
        name="gqa",
    )(gqt, k3, k3, k3, gvt, gvt, gvt, bias3, sink)


def _gqa_mask_bias(nb):
    i = np.arange(BLOCK)[None, :]
    j = np.arange(3 * BLOCK)[:, None]
    band = np.abs(BLOCK + i - j) <= GQA_WINDOW
    first = band & (j >= BLOCK)
    last = band & (j < 2 * BLOCK)
    if nb == 1:
        first = last = first & last
    m = np.stack([first, band, last])
    return jnp.asarray(np.where(m, 0.0, NEG_INF), F32)


def _mixout_kernel(x_ref, ynat_ref, ymla_ref, ygqat_ref, g1_ref, wg_ref, bg_ref, wbr_ref, wo_ref, o_ref):
    x = x_ref[...]
    h = _rms(x, g1_ref[...]).astype(BF16)
    branches = (_dot_tn(ynat_ref[...], wbr_ref[0]), _dot(ymla_ref[...], wbr_ref[1]),
                _dot_tn(ygqat_ref[...], wbr_ref[2]))
    merged = None
    for i, br in enumerate(branches):
        cols = slice(i * D_MODEL, (i + 1) * D_MODEL)
        gate = jax.nn.sigmoid(_dot(h, wg_ref[:, cols]) + bg_ref[:, cols])
        merged = gate * br if merged is None else merged + gate * br
    o_ref[...] = x + _dot(merged.astype(BF16), wo_ref[...])


def _mixout(x2d, y_na_t, y_mla, y_gqa_t, g1, wg, bg, wbr, wo, *, tm):
    T = x2d.shape[0]
    row = lambda w: pl.BlockSpec((tm, w), lambda i: (i, 0))
    col = lambda w: pl.BlockSpec((w, tm), lambda i: (0, i))
    return pl.pallas_call(
        _mixout_kernel,
        grid=(T // tm,),
        in_specs=[row(D_MODEL), col(NA_W), row(MLA_HEADS * MLA_V), col(GQA_Q_W),
                  _const_spec((1, D_MODEL)), _const_spec((D_MODEL, N_BRANCH * D_MODEL)),
                  _const_spec((1, N_BRANCH * D_MODEL)), _const_spec((N_BRANCH, NA_W, D_MODEL)),
                  _const_spec((D_MODEL, D_MODEL))],
        out_specs=row(D_MODEL),
        out_shape=jax.ShapeDtypeStruct((T, D_MODEL), F32),
        compiler_params=_params(1),
        name="mixout",
    )(x2d, y_na_t, y_mla, y_gqa_t, g1, wg, bg, wbr, wo)


def _ffn_kernel(x_ref, xp_ref, xn_ref, g2_ref, wup_ref, cw_ref, cb_ref, wd_ref, fg_ref, o_ref,
                hx_ref, u_ref, acc_ref, *, tm, tiles_per_seq, final):
    j = pl.program_id(0) % tiles_per_seq
    g2 = g2_ref[...]
    x = x_ref[...]
    hp = jnp.where(j == 0, 0.0, _rms(xp_ref[...], g2))
    hn = jnp.where(j == tiles_per_seq - 1, 0.0, _rms(xn_ref[...], g2))
    hx_ref[0:HALO, :] = hp.astype(BF16)
    hx_ref[HALO:HALO + tm, :] = _rms(x, g2).astype(BF16)
    hx_ref[HALO + tm:, :] = hn.astype(BF16)

    def up(c):
        u_ref[c % 2] = _dot(hx_ref[...], wup_ref[c])

    def down(c):
        ub = u_ref.at[c % 2]
        cw = cw_ref[c]
        y = (cw[0:1] * ub[pl.ds(HALO - 1, tm), :] + cw[1:2] * ub[pl.ds(HALO, tm), :]
             + cw[2:3] * ub[pl.ds(HALO + 1, tm), :] + cb_ref[c])
        act = jax.nn.gelu(y[:, FF_CHUNK:]) * y[:, :FF_CHUNK]
        acc_ref[...] += _dot(act.astype(BF16), wd_ref[c])

    acc_ref[...] = jnp.zeros(acc_ref.shape, F32)
    up(0)
    for c in range(N_FF_CHUNKS):
        if c + 1 < N_FF_CHUNKS:
            up(c + 1)
        down(c)
    out = x + acc_ref[...]
    if final:
        out = _rms(out, fg_ref[...])
    o_ref[...] = out


def _ffn(x2d, g2, wup, cw, cb, wd, fg, *, seq, tm, final):
    T = x2d.shape[0]
    tiles_per_seq = seq // tm
    hb = tm // HALO
    n_hb = T // HALO
    return pl.pallas_call(
        functools.partial(_ffn_kernel, tm=tm, tiles_per_seq=tiles_per_seq, final=final),
        grid=(T // tm,),
        in_specs=[pl.BlockSpec((tm, D_MODEL), lambda i: (i, 0)),
                  pl.BlockSpec((HALO, D_MODEL), lambda i: (jnp.maximum(i * hb - 1, 0), 0)),
                  pl.BlockSpec((HALO, D_MODEL), lambda i: (jnp.minimum((i + 1) * hb, n_hb - 1), 0)),
                  _const_spec((1, D_MODEL)),
                  _const_spec((N_FF_CHUNKS, D_MODEL, 2 * FF_CHUNK)),
                  _const_spec((N_FF_CHUNKS, 3, 2 * FF_CHUNK)),
                  _const_spec((N_FF_CHUNKS, 1, 2 * FF_CHUNK)),
                  _const_spec((N_FF_CHUNKS, FF_CHUNK, D_MODEL)),
                  _const_spec((1, D_MODEL))],
        out_specs=pl.BlockSpec((tm, D_MODEL), lambda i: (i, 0)),
        out_shape=jax.ShapeDtypeStruct((T, D_MODEL), F32),
        scratch_shapes=[pltpu.VMEM((tm + 2 * HALO, D_MODEL), BF16),
                        pltpu.VMEM((2, tm + 2 * HALO, 2 * FF_CHUNK), F32),
                        pltpu.VMEM((tm, D_MODEL), F32)],
        compiler_params=_params(1),
        name="ffn",
    )(x2d, x2d, x2d, g2, wup, cw, cb, wd, fg)


def _rope_tables(seq):
    def cs(dim):
        inv = 1.0 / (ROPE_THETA ** (jnp.arange(0, dim, 2, dtype=F32) / dim))
        ang = jnp.arange(seq, dtype=F32)[:, None] * inv[None, :]
        return jnp.cos(ang), jnp.sin(ang)

    def head(c, s):
        z = jnp.zeros_like(s)
        return (jnp.concatenate([c, c], -1), jnp.concatenate([z, s], -1), jnp.concatenate([-s, z], -1))

    m_cos, m_sin = cs(MLA_ROPE)
    mc, ms1, ms2 = head(m_cos, m_sin)
    pad = jnp.zeros_like(mc)
    gc, gs1, gs2 = head(*cs(GQA_HEAD_DIM))
    two = lambda a: jnp.concatenate([a, a], -1)
    table = jnp.concatenate([mc, pad, ms1, pad, ms2, pad, two(gc), two(gs1), two(gs2)], axis=-1)
    table_t = jnp.concatenate([m_cos, m_sin], axis=-1).T
    return table, table_t


def _prep_layer(w_in, b_gate, mla_w_uq, mla_w_ukv, w_br_na, w_br_mla, w_br_gqa, w_out,
                w_up, conv_w, conv_b, w_down):
    o_nak = NA_W
    o_nav = 2 * NA_W
    o_cq = 3 * NA_W
    o_ckv = o_cq + MLA_Q_RANK
    o_kr = o_ckv + MLA_KV_RANK
    o_gq = o_kr + MLA_ROPE
    o_gkv = o_gq + GQA_Q_W
    o_gv = o_gkv + GQA_KV_W
    o_gate = o_gv + GQA_KV_W
    na_scale = NA_HEAD_DIM ** -0.5
    gqa_scale = GQA_HEAD_DIM ** -0.5
    w1 = jnp.concatenate([
        w_in[:, o_nak:o_nav], w_in[:, o_cq:o_ckv], w_in[:, o_ckv:o_kr], w_in[:, o_gkv:o_gv],
        w_in[:, o_kr:o_gq], jnp.zeros((D_MODEL, LANES - MLA_ROPE), w_in.dtype)], axis=1).astype(BF16)
    wgt = jnp.concatenate([w_in[:, 0:o_nak] * na_scale, w_in[:, o_nav:o_cq],
                           w_in[:, o_gq:o_gkv] * gqa_scale, w_in[:, o_gv:o_gate]], axis=1).T.astype(BF16)
    wg = w_in[:, o_gate:].astype(BF16)
    wuqt = jnp.pad(mla_w_uq.reshape(MLA_Q_RANK, MLA_HEADS, MLA_QK),
                   ((0, 0), (0, 0), (0, MLA_QK_PAD - MLA_QK))).reshape(MLA_Q_RANK, -1).T.astype(BF16)
    ukv = mla_w_ukv.reshape(MLA_KV_RANK, MLA_HEADS, MLA_NOPE + MLA_V)
    wuk = ukv[:, :, :MLA_NOPE].reshape(MLA_KV_RANK, -1).astype(BF16)
    wuvt = ukv[:, :, MLA_NOPE:].reshape(MLA_KV_RANK, -1).T.astype(BF16)
    wbr = jnp.stack([w_br_na, w_br_mla, w_br_gqa]).astype(BF16)
    chunks = lambda a: a.reshape(a.shape[0], 2, N_FF_CHUNKS, FF_CHUNK)
    wup = chunks(w_up).transpose(2, 0, 1, 3).reshape(N_FF_CHUNKS, D_MODEL, 2 * FF_CHUNK).astype(BF16)
    cw = chunks(conv_w).transpose(2, 0, 1, 3).reshape(N_FF_CHUNKS, 3, 2 * FF_CHUNK)
    cb = chunks(conv_b[None]).transpose(2, 0, 1, 3).reshape(N_FF_CHUNKS, 1, 2 * FF_CHUNK)
    wd = w_down.reshape(N_FF_CHUNKS, FF_CHUNK, D_MODEL).astype(BF16)
    return dict(w1=w1, wgt=wgt, wg=wg, bg=b_gate[None], wuqt=wuqt, wuk=wuk, wuvt=wuvt, wbr=wbr,
                wo=w_out.astype(BF16), wup=wup, cw=cw, cb=cb, wd=wd)


def _tiles(seq):
    tm = min(512, seq)
    tq = min(2048, seq)
    return tm, tq


def kernel(x, norm1_g, w_in, b_gate, na_rpb, mla_qa_g, mla_kva_g, mla_w_uq, mla_w_ukv, gqa_sink, w_br_na, w_br_mla, w_br_gqa, w_out, norm2_g, w_up, conv_w, conv_b, w_down, final_g):
    batch, seq, _ = x.shape
    depth = w_in.shape[0]
    rows = seq // GRID_W
    assert seq % (NA_ROWS_PER_STEP * GRID_W) == 0 and rows >= 12 and rows % 2 == 0
    assert seq % (GQA_BLOCKS_PER_STEP * BLOCK) == 0 and MLA_ROPE == GQA_HEAD_DIM
    tm, tq = _tiles(seq)
    assert tm % MLA_KC == 0 and seq % tq == 0
    rope, rope_t = _rope_tables(seq)
    gqa_bias = _gqa_mask_bias(seq // BLOCK)
    xf = x.reshape(batch * seq, D_MODEL)
    for l in range(depth):
        p = _prep_layer(w_in[l], b_gate[l], mla_w_uq[l], mla_w_ukv[l], w_br_na[l], w_br_mla[l],
                        w_br_gqa[l], w_out[l], w_up[l], conv_w[l], conv_b[l], w_down[l])
        g1 = norm1_g[l][None]
        naqt, nak, navt, mqt, mk, mvt, gqt, gk, gvt = _inproj(
            xf, g1, p["w1"], mla_qa_g[l][None], mla_kva_g[l][None], p["wuqt"], p["wuk"], p["wuvt"],
            p["wgt"], rope, rope_t, seq=seq, tm=tm)
        y_na_t = _na(naqt, nak, navt, _na_bias_table(na_rpb[l], rows), batch=batch, seq=seq)
        y_mla = _mla(mqt, mk, mvt, batch=batch, seq=seq, tq=tq).reshape(batch * seq, -1)
        sink = jnp.repeat(gqa_sink[l].astype(F32).reshape(GQA_KV_HEADS, 1, GQA_GROUP), BLOCK, axis=-1)
        y_gqa_t = _gqa(gqt, gk, gvt, gqa_bias, sink, batch=batch, seq=seq)
        xf = _mixout(xf, y_na_t, y_mla, y_gqa_t, g1, p["wg"], p["bg"], p["wbr"], p["wo"], tm=tm)
        xf = _ffn(xf, norm2_g[l][None], p["wup"], p["cw"], p["cb"], p["wd"], final_g[None],
                  seq=seq, tm=tm, final=(l == depth - 1))
    return xf.reshape(batch, seq, D_MODEL)
```

```python
import functools

import jax
import jax.numpy as jnp
import numpy as np
from jax import lax
from jax.experimental import pallas as pl
from jax.experimental.pallas import tpu as pltpu

F32 = jnp.float32
BF16 = jnp.bfloat16

D_MODEL = 1024
GRID_W = 64
NA_HEADS = 8
NA_HEAD_DIM = 64
NA_WIN_R = 8
NA_WIN_C = 16
NA_W = NA_HEADS * NA_HEAD_DIM
MLA_HEADS = 4
MLA_Q_RANK = 384
MLA_KV_RANK = 256
MLA_NOPE = 128
MLA_ROPE = 64
MLA_V = 128
MLA_QK = MLA_NOPE + MLA_ROPE
MLA_QK_PAD = 256
MLA_KC = 512
MLA_VT_ROWS = MLA_V + 16
MLA_AHEAD = 1
MLA_LOG2_SCALE = float(MLA_QK ** -0.5 * np.log2(np.e))
GQA_HEADS = 8
GQA_KV_HEADS = 2
GQA_GROUP = GQA_HEADS // GQA_KV_HEADS
GQA_HEAD_DIM = 64
GQA_WINDOW = 128
GQA_Q_W = GQA_HEADS * GQA_HEAD_DIM
GQA_KV_W = GQA_KV_HEADS * GQA_HEAD_DIM
BLOCK = 128
N_BRANCH = 3
D_FF = 2816
ROPE_THETA = 10000.0
EPS = 1e-6
NEG_INF = -1e30

LANES = 128
ONES_ROWS = 16
IN1_W = 1408
IN2_W = 1664
FF_CHUNK = 256
N_FF_CHUNKS = D_FF // FF_CHUNK
HALO = 16
VMEM_LIMIT = 56 * 1024 * 1024

_O_NAK = 0
_O_CQ = 512
_O_CKV = 896
_O_GK = 1152
_O_KR = 1280
_T_NAQ = 0
_T_NAV = 512
_T_GQ = 1024
_T_GV = 1536


def _rms(x, g):
    return x * lax.rsqrt(jnp.mean(x * x, axis=-1, keepdims=True) + EPS) * g


def _rope128(blk, c, s1, s2):
    return blk * c + pltpu.roll(blk, 32, 1) * s1 + pltpu.roll(blk, 96, 1) * s2


def _dot(a, b):
    return jnp.dot(a, b, preferred_element_type=F32)


def _dot_nt(a, b):
    return lax.dot_general(a, b, (((1,), (1,)), ((), ())), preferred_element_type=F32)


def _dot_tn(a, b):
    return lax.dot_general(a, b, (((0,), (0,)), ((), ())), preferred_element_type=F32)


def _ones_rows(n):
    return (lax.broadcasted_iota(jnp.int32, (ONES_ROWS, n), 0) == 0).astype(BF16)


def _params(n_axes):
    return pltpu.CompilerParams(dimension_semantics=("arbitrary",) * n_axes,
                                vmem_limit_bytes=VMEM_LIMIT)


def _const_spec(shape):
    n = len(shape)
    return pl.BlockSpec(shape, lambda *_: (0,) * n, pipeline_mode=pl.Buffered(1))


def _inproj_kernel(x_ref, g1_ref, w1_ref, gqa_ref, gkva_ref, wuqt_ref, wuk_ref, wuvt_ref, wgt_ref,
                   rope_ref, ropet_ref, naqt_ref, nak_ref, navt_ref, mqt_ref, mk_ref, mvt_ref,
                   gqt_ref, gk_ref, gvt_ref):
    tm = x_ref.shape[0]
    h = _rms(x_ref[...], g1_ref[...]).astype(BF16)
    z = _dot(h, w1_ref[...])
    gt = _dot_nt(wgt_ref[...], h)

    naqt_ref[...] = gt[_T_NAQ:_T_NAV, :].astype(BF16)
    for c in range(tm // LANES):
        navt_ref[c] = gt[_T_NAV:_T_GQ, c * LANES:(c + 1) * LANES].astype(BF16)
    nak_ref[...] = z[:, _O_NAK:_O_CQ].astype(BF16)

    mc, ms1, ms2 = rope_ref[:, 0:128], rope_ref[:, 128:256], rope_ref[:, 256:384]
    gc, gs1, gs2 = rope_ref[:, 384:512], rope_ref[:, 512:640], rope_ref[:, 640:768]

    cqn = _rms(z[:, _O_CQ:_O_CKV], gqa_ref[...]).astype(BF16)
    qt = _dot_nt(wuqt_ref[...], cqn) * MLA_LOG2_SCALE
    half = MLA_ROPE // 2
    ct, st = ropet_ref[0:half, :], ropet_ref[half:MLA_ROPE, :]
    for hd in range(MLA_HEADS):
        o = hd * MLA_QK_PAD
        r = o + MLA_NOPE
        mqt_ref[o:r, :] = qt[o:r, :].astype(BF16)
        x1, x2 = qt[r:r + half, :], qt[r + half:r + MLA_ROPE, :]
        mqt_ref[r:r + half, :] = (x1 * ct - x2 * st).astype(BF16)
        mqt_ref[r + half:r + MLA_ROPE, :] = (x1 * st + x2 * ct).astype(BF16)
        mqt_ref[r + MLA_ROPE:o + MLA_QK_PAD, :] = qt[r + MLA_ROPE:o + MLA_QK_PAD, :].astype(BF16)

    ckvn = _rms(z[:, _O_CKV:_O_GK], gkva_ref[...]).astype(BF16)
    kn = _dot(ckvn, wuk_ref[...])
    kpe = _rope128(z[:, _O_KR:_O_KR + LANES], mc, ms1, ms2).astype(BF16)
    for hd in range(MLA_HEADS):
        o = hd * MLA_QK_PAD
        mk_ref[:, o:o + LANES] = kn[:, hd * LANES:(hd + 1) * LANES].astype(BF16)
        mk_ref[:, o + LANES:o + 2 * LANES] = kpe
    vt = _dot_nt(wuvt_ref[...], ckvn)
    ones_rows = _ones_rows(MLA_KC)
    for hd in range(MLA_HEADS):
        for c in range(tm // MLA_KC):
            mvt_ref[hd, c, 0:MLA_V, :] = vt[hd * MLA_V:(hd + 1) * MLA_V,
                                            c * MLA_KC:(c + 1) * MLA_KC].astype(BF16)
            mvt_ref[hd, c, MLA_V:MLA_VT_ROWS, :] = ones_rows

    for hd in range(GQA_HEADS):
        r = _T_GQ + hd * GQA_HEAD_DIM
        x1, x2 = gt[r:r + half, :], gt[r + half:r + GQA_HEAD_DIM, :]
        o = hd * GQA_HEAD_DIM
        gqt_ref[o:o + half, :] = (x1 * ct - x2 * st).astype(BF16)
        gqt_ref[o + half:o + GQA_HEAD_DIM, :] = (x1 * st + x2 * ct).astype(BF16)
    gvt_ref[...] = gt[_T_GV:, :].astype(BF16)
    gk_ref[...] = _rope128(z[:, _O_GK:_O_KR], gc, gs1, gs2).astype(BF16)


def _inproj(x2d, g1, w1, g_qa, g_kva, wuqt, wuk, wuvt, wgt, rope, ropet, *, seq, tm):
    T = x2d.shape[0]
    tiles_per_seq = seq // tm
    cpt = tm // MLA_KC
    row = lambda w: pl.BlockSpec((tm, w), lambda i: (i, 0))
    col = lambda w: pl.BlockSpec((w, tm), lambda i: (0, i))
    qk_w = MLA_HEADS * MLA_QK_PAD
    out_specs = [col(NA_W), row(NA_W),
                 pl.BlockSpec((tm // LANES, NA_W, LANES), lambda i: (i, 0, 0)),
                 col(qk_w), row(qk_w),
                 pl.BlockSpec((MLA_HEADS, cpt, MLA_VT_ROWS, MLA_KC), lambda i: (0, i, 0, 0)),
                 col(GQA_Q_W), row(GQA_KV_W), col(GQA_KV_W)]
    out_shape = [jax.ShapeDtypeStruct((NA_W, T), BF16),
                 jax.ShapeDtypeStruct((T, NA_W), BF16),
                 jax.ShapeDtypeStruct((T // LANES, NA_W, LANES), BF16),
                 jax.ShapeDtypeStruct((qk_w, T), BF16),
                 jax.ShapeDtypeStruct((T, qk_w), BF16),
                 jax.ShapeDtypeStruct((MLA_HEADS, T // MLA_KC, MLA_VT_ROWS, MLA_KC), BF16),
                 jax.ShapeDtypeStruct((GQA_Q_W, T), BF16),
                 jax.ShapeDtypeStruct((T, GQA_KV_W), BF16),
                 jax.ShapeDtypeStruct((GQA_KV_W, T), BF16)]
    return pl.pallas_call(
        _inproj_kernel,
        grid=(T // tm,),
        in_specs=[row(D_MODEL), _const_spec((1, D_MODEL)), _const_spec((D_MODEL, IN1_W)),
                  _const_spec((1, MLA_Q_RANK)), _const_spec((1, MLA_KV_RANK)),
                  _const_spec((qk_w, MLA_Q_RANK)),
                  _const_spec((MLA_KV_RANK, MLA_HEADS * MLA_NOPE)),
                  _const_spec((MLA_HEADS * MLA_V, MLA_KV_RANK)),
                  _const_spec((IN2_W, D_MODEL)),
                  pl.BlockSpec((tm, 6 * LANES), lambda i: (i % tiles_per_seq, 0)),
                  pl.BlockSpec((MLA_ROPE, tm), lambda i: (0, i % tiles_per_seq))],
        out_specs=out_specs,
        out_shape=out_shape,
        compiler_params=_params(1),
        name="inproj",
    )(x2d, g1, w1, g_qa, g_kva, wuqt, wuk, wuvt, wgt, rope, ropet)


NA_ROWS_PER_STEP = 16
NA_KEY_ROWS = 10
NA_CASES = 5
NA_AHEAD = 3


def _na_kernel(qt_ref, k_ref, vt_ref, tb_ref, o_ref, *, rows):
    step = pl.program_id(1)
    n_keys = NA_KEY_ROWS * GRID_W
    ones_rows = _ones_rows(n_keys)
    zeros = jnp.zeros((NA_HEAD_DIM, LANES), BF16)
    pairs = []
    for rp in range(NA_ROWS_PER_STEP // 2):
        r = step * NA_ROWS_PER_STEP + 2 * rp
        case = jnp.where(r == 0, 0, jnp.where(r == 2, 1, jnp.where(
            r == rows - 4, 3, jnp.where(r == rows - 2, 4, 2))))
        u = jnp.clip(r - NA_WIN_R // 2, 0, rows - NA_KEY_ROWS)
        pairs.append((case, pl.multiple_of(u * GRID_W, LANES), u // 2))

    def scores(rp, hp):
        case, tok0, _ = pairs[rp]
        lo, mid, hi = hp * LANES, hp * LANES + NA_HEAD_DIM, (hp + 1) * LANES
        qcols = slice(rp * LANES, (rp + 1) * LANES)
        qa, qb = qt_ref[lo:mid, qcols], qt_ref[mid:hi, qcols]
        rhs = jnp.concatenate([jnp.concatenate([qa, zeros], axis=1),
                               jnp.concatenate([zeros, qb], axis=1)], axis=0)
        bias = jnp.concatenate([tb_ref[case, 2 * hp], tb_ref[case, 2 * hp + 1]], axis=1)
        return _dot(k_ref[pl.ds(tok0, n_keys), lo:hi], rhs) + bias

    def finish(rp, hp, t):
        _, _, c0 = pairs[rp]
        lo, mid, hi = hp * LANES, hp * LANES + NA_HEAD_DIM, (hp + 1) * LANES
        qcols = slice(rp * LANES, (rp + 1) * LANES)
        m = jnp.max(t, axis=0, keepdims=True)
        p = jnp.exp(t - m).astype(BF16)
        vt_win = jnp.concatenate([vt_ref[c0 + j, lo:hi, :] for j in range(n_keys // LANES)], axis=1)
        lhs = jnp.concatenate([vt_win, ones_rows], axis=0)
        acc = _dot(lhs, p)
        l = acc[LANES:LANES + 1, :]
        o_ref[lo:mid, qcols] = (acc[0:NA_HEAD_DIM, 0:LANES] / l[:, 0:LANES]).astype(BF16)
        o_ref[mid:hi, qcols] = (acc[NA_HEAD_DIM:LANES, LANES:] / l[:, LANES:]).astype(BF16)

    units = [(rp, hp) for rp in range(NA_ROWS_PER_STEP // 2) for hp in range(NA_HEADS // 2)]
    pending = [scores(*u) for u in units[:NA_AHEAD]]
    for idx, unit in enumerate(units):
        if idx + NA_AHEAD < len(units):
            pending.append(scores(*units[idx + NA_AHEAD]))
        finish(*unit, pending.pop(0))


def _na(naqt, nak, navt, tb, *, batch, seq):
    rows = seq // GRID_W
    tq = NA_ROWS_PER_STEP * GRID_W
    n_steps = seq // tq
    T = batch * seq
    k3 = nak.reshape(batch, seq, NA_W)
    vt4 = navt.reshape(batch, seq // LANES, NA_W, LANES)
    return pl.pallas_call(
        functools.partial(_na_kernel, rows=rows),
        grid=(batch, n_steps),
        in_specs=[pl.BlockSpec((NA_W, tq), lambda b, i: (0, b * n_steps + i)),
                  pl.BlockSpec((None, seq, NA_W), lambda b, i: (b, 0, 0), pipeline_mode=pl.Buffered(1)),
                  pl.BlockSpec((None, seq // LANES, NA_W, LANES), lambda b, i: (b, 0, 0, 0),
                               pipeline_mode=pl.Buffered(1)),
                  _const_spec(tb.shape)],
        out_specs=pl.BlockSpec((NA_W, tq), lambda b, i: (0, b * n_steps + i)),
        out_shape=jax.ShapeDtypeStruct((NA_W, T), BF16),
        compiler_params=_params(2),
        name="na",
    )(naqt, k3, vt4, tb)


def _na_bias_table(rpb, rows):
    kc = np.arange(GRID_W)[:, None]
    c = np.arange(GRID_W)[None, :]
    c_start = np.clip(c - NA_WIN_C // 2, 0, GRID_W - NA_WIN_C)
    valid_c = (kc >= c_start) & (kc < c_start + NA_WIN_C)
    side = GRID_W - NA_WIN_C
    padded = jnp.pad(rpb.astype(F32), ((0, 0), (0, 0), (side, side)))
    toe = jnp.stack([padded[:, :, GRID_W - 1 - cc:2 * GRID_W - 1 - cc] for cc in range(GRID_W)], axis=-1)
    toe = jnp.where(valid_c[None, None], toe, NEG_INF)
    neg = jnp.full((rpb.shape[0], GRID_W, GRID_W), NEG_INF, F32)
    tabs = []
    for r in (0, 2, 4, rows - 4, rows - 2):
        u = min(max(r - NA_WIN_R // 2, 0), rows - NA_KEY_ROWS)
        key_rows = []
        for i in range(NA_KEY_ROWS):
            blocks = []
            for rr in range(2):
                r_start = min(max(r + rr - NA_WIN_R // 2, 0), rows - NA_WIN_R)
                inside = r_start <= u + i < r_start + NA_WIN_R
                blocks.append(toe[:, u + i - (r + rr) + (NA_WIN_R - 1)] if inside else neg)
            key_rows.append(jnp.concatenate(blocks, axis=-1))
        tabs.append(jnp.concatenate(key_rows, axis=1))
    return jnp.stack(tabs)


def _mla_kernel(qt_ref, k_ref, vt_ref, o_ref, *, n_chunks):
    qt = qt_ref[...]
    tq = qt.shape[1]

    def scores(c):
        k0 = pl.multiple_of(c * MLA_KC, MLA_KC)
        return _dot(k_ref[pl.ds(k0, MLA_KC), :], qt)

    def update(c, m, acc, t):
        m_new = jnp.maximum(m, jnp.max(t, axis=0, keepdims=True))
        alpha = jnp.exp2(m - m_new)
        p = jnp.exp2(t - m_new).astype(BF16)
        return m_new, alpha * acc + _dot(vt_ref[c], p)

    m = jnp.full((1, tq), -jnp.inf, F32)
    acc = jnp.zeros((MLA_VT_ROWS, tq), F32)
    pending = [scores(c) for c in range(MLA_AHEAD)]
    for c in range(n_chunks):
        if c + MLA_AHEAD < n_chunks:
            pending.append(scores(c + MLA_AHEAD))
        m, acc = update(c, m, acc, pending.pop(0))
    o = acc[0:MLA_V, :] / acc[MLA_V:MLA_V + 1, :]
    o_ref[...] = o.T.astype(BF16)


def _mla(mqt, mk, mvt, *, batch, seq, tq):
    n_chunks = seq // MLA_KC
    nq = seq // tq
    k3 = mk.reshape(batch, seq, MLA_HEADS * MLA_QK_PAD)
    return pl.pallas_call(
        functools.partial(_mla_kernel, n_chunks=n_chunks),
        grid=(batch, MLA_HEADS, nq),
        in_specs=[pl.BlockSpec((MLA_QK_PAD, tq), lambda b, h, i: (h, b * nq + i)),
                  pl.BlockSpec((None, seq, MLA_QK_PAD), lambda b, h, i: (b, 0, h)),
                  pl.BlockSpec((None, n_chunks, MLA_VT_ROWS, MLA_KC), lambda b, h, i: (h, b, 0, 0))],
        out_specs=pl.BlockSpec((None, tq, MLA_V), lambda b, h, i: (b, i, h)),
        out_shape=jax.ShapeDtypeStruct((batch, seq, MLA_HEADS * MLA_V), BF16),
        compiler_params=_params(3),
        name="mla",
    )(mqt, k3, mvt)


GQA_BLOCKS_PER_STEP = 4
GQA_AHEAD = 1


def _gqa_kernel(qt_ref, kp_ref, kc_ref, kn_ref, vp_ref, vc_ref, vn_ref, b_ref, sk_ref, o_ref, *,
                n_steps):
    i = pl.program_id(1)
    nbs = GQA_BLOCKS_PER_STEP
    k_all = jnp.concatenate([kp_ref[...], kc_ref[...], kn_ref[...]], axis=0)
    vt_all = jnp.concatenate([vp_ref[...], vc_ref[...], vn_ref[...]], axis=1)
    ones_rows = _ones_rows(3 * BLOCK)
    zeros = jnp.zeros((GQA_HEAD_DIM, GQA_GROUP * BLOCK), BF16)

    def scores(j, kvh):
        if j == 0:
            bias = b_ref[jnp.where(i == 0, 0, 1)]
        elif j == nbs - 1:
            bias = b_ref[jnp.where(i == n_steps - 1, 2, 1)]
        else:
            bias = b_ref[1]
        bias = jnp.concatenate([bias] * GQA_GROUP, axis=1)
        heads = range(kvh * GQA_GROUP, (kvh + 1) * GQA_GROUP)
        qs = jnp.concatenate([qt_ref[hd * GQA_HEAD_DIM:(hd + 1) * GQA_HEAD_DIM,
                                     j * BLOCK:(j + 1) * BLOCK] for hd in heads], axis=1)
        rhs = jnp.concatenate([qs, zeros] if kvh == 0 else [zeros, qs], axis=0)
        return _dot(k_all[j * BLOCK:(j + 3) * BLOCK, :], rhs) + bias

    def finish(j, kvh, t):
        sink = sk_ref[kvh]
        m = jnp.maximum(jnp.max(t, axis=0, keepdims=True), sink)
        p = jnp.exp(t - m).astype(BF16)
        vt2 = vt_all[kvh * GQA_HEAD_DIM:(kvh + 1) * GQA_HEAD_DIM, j * BLOCK:(j + 3) * BLOCK]
        acc = _dot(jnp.concatenate([vt2, ones_rows], axis=0), p)
        l = acc[GQA_HEAD_DIM:GQA_HEAD_DIM + 1, :] + jnp.exp(sink - m)
        o = acc[0:GQA_HEAD_DIM, :] / l
        for g in range(GQA_GROUP):
            hd = kvh * GQA_GROUP + g
            o_ref[hd * GQA_HEAD_DIM:(hd + 1) * GQA_HEAD_DIM, j * BLOCK:(j + 1) * BLOCK] = (
                o[:, g * BLOCK:(g + 1) * BLOCK].astype(BF16))

    units = [(j, kvh) for j in range(nbs) for kvh in range(GQA_KV_HEADS)]
    pending = [scores(*u) for u in units[:GQA_AHEAD]]
    for idx, unit in enumerate(units):
        if idx + GQA_AHEAD < len(units):
            pending.append(scores(*units[idx + GQA_AHEAD]))
        finish(*unit, pending.pop(0))


def _gqa(gqt, gk, gvt, bias3, sink, *, batch, seq):
    nb = seq // BLOCK
    nbs = GQA_BLOCKS_PER_STEP
    n_steps = nb // nbs
    T = batch * seq
    k3 = gk.reshape(batch, seq, GQA_KV_W)
    prev = lambda i: jnp.maximum(i * nbs - 1, 0)
    nxt = lambda i: jnp.minimum((i + 1) * nbs, nb - 1)
    return pl.pallas_call(
        functools.partial(_gqa_kernel, n_steps=n_steps),
        grid=(batch, n_steps),
        in_specs=[pl.BlockSpec((GQA_Q_W, nbs * BLOCK), lambda b, i: (0, b * n_steps + i)),
                  pl.BlockSpec((None, BLOCK, GQA_KV_W), lambda b, i: (b, prev(i), 0)),
                  pl.BlockSpec((None, nbs * BLOCK, GQA_KV_W), lambda b, i: (b, i, 0)),
                  pl.BlockSpec((None, BLOCK, GQA_KV_W), lambda b, i: (b, nxt(i), 0)),
                  pl.BlockSpec((GQA_KV_W, BLOCK), lambda b, i: (0, b * nb + prev(i))),
                  pl.BlockSpec((GQA_KV_W, nbs * BLOCK), lambda b, i: (0, b * n_steps + i)),
                  pl.BlockSpec((GQA_KV_W, BLOCK), lambda b, i: (0, b * nb + nxt(i))),
                  _const_spec((3, 3 * BLOCK, BLOCK)),
                  _const_spec((GQA_KV_HEADS, 1, GQA_GROUP * BLOCK))],
        out_specs=pl.BlockSpec((GQA_Q_W, nbs * BLOCK), lambda b, i: (0, b * n_steps + i)),
        out_shape=jax.ShapeDtypeStruct((GQA_Q_W, T), BF16),
        compiler_params=_params(2),
        name="gqa",
    )(gqt, k3, k3, k3, gvt, gvt, gvt, bias3, sink)


def _gqa_mask_bias(nb):
    i = np.arange(BLOCK)[None, :]
    j = np.arange(3 * BLOCK)[:, None]
    band = np.abs(BLOCK + i - j) <= GQA_WINDOW
    first = band & (j >= BLOCK)
    last = band & (j < 2 * BLOCK)
    if nb == 1:
        first = last = first & last
    m = np.stack([first, band, last])
    return jnp.asarray(np.where(m, 0.0, NEG_INF), F32)


def _mixout_kernel(x_ref, ynat_ref, ymla_ref, ygqat_ref, g1_ref, wg_ref, bg_ref, wbr_ref, wo_ref, o_ref):
    x = x_ref[...]
    h = _rms(x, g1_ref[...]).astype(BF16)
    branches = (_dot_tn(ynat_ref[...], wbr_ref[0]), _dot(ymla_ref[...], wbr_ref[1]),
                _dot_tn(ygqat_ref[...], wbr_ref[2]))
    merged = None
    for i, br in enumerate(branches):
        cols = slice(i * D_MODEL, (i + 1) * D_MODEL)
        gate = jax.nn.sigmoid(_dot(h, wg_ref[:, cols]) + bg_ref[:, cols])
        merged = gate * br if merged is None else merged + gate * br
    o_ref[...] = x + _dot(merged.astype(BF16), wo_ref[...])


def _mixout(x2d, y_na_t, y_mla, y_gqa_t, g1, wg, bg, wbr, wo, *, tm):
    T = x2d.shape[0]
    row = lambda w: pl.BlockSpec((tm, w), lambda i: (i, 0))
    col = lambda w: pl.BlockSpec((w, tm), lambda i: (0, i))
    return pl.pallas_call(
        _mixout_kernel,
        grid=(T // tm,),
        in_specs=[row(D_MODEL), col(NA_W), row(MLA_HEADS * MLA_V), col(GQA_Q_W),
                  _const_spec((1, D_MODEL)), _const_spec((D_MODEL, N_BRANCH * D_MODEL)),
                  _const_spec((1, N_BRANCH * D_MODEL)), _const_spec((N_BRANCH, NA_W, D_MODEL)),
                  _const_spec((D_MODEL, D_MODEL))],
        out_specs=row(D_MODEL),
        out_shape=jax.ShapeDtypeStruct((T, D_MODEL), F32),
        compiler_params=_params(1),
        name="mixout",
    )(x2d, y_na_t, y_mla, y_gqa_t, g1, wg, bg, wbr, wo)


def _ffn_kernel(x_ref, xp_ref, xn_ref, g2_ref, wup_ref, cw_ref, cb_ref, wd_ref, fg_ref, o_ref,
                hx_ref, u_ref, acc_ref, *, tm, tiles_per_seq, final):
    j = pl.program_id(0) % tiles_per_seq
    g2 = g2_ref[...]
    x = x_ref[...]
    hp = jnp.where(j == 0, 0.0, _rms(xp_ref[...], g2))
    hn = jnp.where(j == tiles_per_seq - 1, 0.0, _rms(xn_ref[...], g2))
    hx_ref[0:HALO, :] = hp.astype(BF16)
    hx_ref[HALO:HALO + tm, :] = _rms(x, g2).astype(BF16)
    hx_ref[HALO + tm:, :] = hn.astype(BF16)

    def up(c):
        u_ref[c % 2] = _dot(hx_ref[...], wup_ref[c])

    def down(c):
        ub = u_ref.at[c % 2]
        cw = cw_ref[c]
        y = (cw[0:1] * ub[pl.ds(HALO - 1, tm), :] + cw[1:2] * ub[pl.ds(HALO, tm), :]
             + cw[2:3] * ub[pl.ds(HALO + 1, tm), :] + cb_ref[c])
        act = jax.nn.gelu(y[:, FF_CHUNK:]) * y[:, :FF_CHUNK]
        acc_ref[...] += _dot(act.astype(BF16), wd_ref[c])

    acc_ref[...] = jnp.zeros(acc_ref.shape, F32)
    up(0)
    for c in range(N_FF_CHUNKS):
        if c + 1 < N_FF_CHUNKS:
            up(c + 1)
        down(c)
    out = x + acc_ref[...]
    if final:
        out = _rms(out, fg_ref[...])
    o_ref[...] = out


def _ffn(x2d, g2, wup, cw, cb, wd, fg, *, seq, tm, final):
    T = x2d.shape[0]
    tiles_per_seq = seq // tm
    hb = tm // HALO
    n_hb = T // HALO
    return pl.pallas_call(
        functools.partial(_ffn_kernel, tm=tm, tiles_per_seq=tiles_per_seq, final=final),
        grid=(T // tm,),
        in_specs=[pl.BlockSpec((tm, D_MODEL), lambda i: (i, 0)),
                  pl.BlockSpec((HALO, D_MODEL), lambda i: (jnp.maximum(i * hb - 1, 0), 0)),
                  pl.BlockSpec((HALO, D_MODEL), lambda i: (jnp.minimum((i + 1) * hb, n_hb - 1), 0)),
                  _const_spec((1, D_MODEL)),
                  _const_spec((N_FF_CHUNKS, D_MODEL, 2 * FF_CHUNK)),
                  _const_spec((N_FF_CHUNKS, 3, 2 * FF_CHUNK)),
                  _const_spec((N_FF_CHUNKS, 1, 2 * FF_CHUNK)),
                  _const_spec((N_FF_CHUNKS, FF_CHUNK, D_MODEL)),
                  _const_spec((1, D_MODEL))],
        out_specs=pl.BlockSpec((tm, D_MODEL), lambda i: (i, 0)),
        out_shape=jax.ShapeDtypeStruct((T, D_MODEL), F32),
        scratch_shapes=[pltpu.VMEM((tm + 2 * HALO, D_MODEL), BF16),
                        pltpu.VMEM((2, tm + 2 * HALO, 2 * FF_CHUNK), F32),
                        pltpu.VMEM((tm, D_MODEL), F32)],
        compiler_params=_params(1),
        name="ffn",
    )(x2d, x2d, x2d, g2, wup, cw, cb, wd, fg)


def _rope_tables(seq):
    def cs(dim):
        inv = 1.0 / (ROPE_THETA ** (jnp.arange(0, dim, 2, dtype=F32) / dim))
        ang = jnp.arange(seq, dtype=F32)[:, None] * inv[None, :]
        return jnp.cos(ang), jnp.sin(ang)

    def head(c, s):
        z = jnp.zeros_like(s)
        return (jnp.concatenate([c, c], -1), jnp.concatenate([z, s], -1), jnp.concatenate([-s, z], -1))

    m_cos, m_sin = cs(MLA_ROPE)
    mc, ms1, ms2 = head(m_cos, m_sin)
    pad = jnp.zeros_like(mc)
    gc, gs1, gs2 = head(*cs(GQA_HEAD_DIM))
    two = lambda a: jnp.concatenate([a, a], -1)
    table = jnp.concatenate([mc, pad, ms1, pad, ms2, pad, two(gc), two(gs1), two(gs2)], axis=-1)
    table_t = jnp.concatenate([m_cos, m_sin], axis=-1).T
    return table, table_t


def _prep_layer(w_in, b_gate, mla_w_uq, mla_w_ukv, w_br_na, w_br_mla, w_br_gqa, w_out,
                w_up, conv_w, conv_b, w_down):
    o_nak = NA_W
    o_nav = 2 * NA_W
    o_cq = 3 * NA_W
    o_ckv = o_cq + MLA_Q_RANK
    o_kr = o_ckv + MLA_KV_RANK
    o_gq = o_kr + MLA_ROPE
    o_gkv = o_gq + GQA_Q_W
    o_gv = o_gkv + GQA_KV_W
    o_gate = o_gv + GQA_KV_W
    na_scale = NA_HEAD_DIM ** -0.5
    gqa_scale = GQA_HEAD_DIM ** -0.5
    w1 = jnp.concatenate([
        w_in[:, o_nak:o_nav], w_in[:, o_cq:o_ckv], w_in[:, o_ckv:o_kr], w_in[:, o_gkv:o_gv],
        w_in[:, o_kr:o_gq], jnp.zeros((D_MODEL, LANES - MLA_ROPE), w_in.dtype)], axis=1).astype(BF16)
    wgt = jnp.concatenate([w_in[:, 0:o_nak] * na_scale, w_in[:, o_nav:o_cq],
                           w_in[:, o_gq:o_gkv] * gqa_scale, w_in[:, o_gv:o_gate]], axis=1).T.astype(BF16)
    wg = w_in[:, o_gate:].astype(BF16)
    wuqt = jnp.pad(mla_w_uq.reshape(MLA_Q_RANK, MLA_HEADS, MLA_QK),
                   ((0, 0), (0, 0), (0, MLA_QK_PAD - MLA_QK))).reshape(MLA_Q_RANK, -1).T.astype(BF16)
    ukv = mla_w_ukv.reshape(MLA_KV_RANK, MLA_HEADS, MLA_NOPE + MLA_V)
    wuk = ukv[:, :, :MLA_NOPE].reshape(MLA_KV_RANK, -1).astype(BF16)
    wuvt = ukv[:, :, MLA_NOPE:].reshape(MLA_KV_RANK, -1).T.astype(BF16)
    wbr = jnp.stack([w_br_na, w_br_mla, w_br_gqa]).astype(BF16)
    chunks = lambda a: a.reshape(a.shape[0], 2, N_FF_CHUNKS, FF_CHUNK)
    wup = chunks(w_up).transpose(2, 0, 1, 3).reshape(N_FF_CHUNKS, D_MODEL, 2 * FF_CHUNK).astype(BF16)
    cw = chunks(conv_w).transpose(2, 0, 1, 3).reshape(N_FF_CHUNKS, 3, 2 * FF_CHUNK)
    cb = chunks(conv_b[None]).transpose(2, 0, 1, 3).reshape(N_FF_CHUNKS, 1, 2 * FF_CHUNK)
    wd = w_down.reshape(N_FF_CHUNKS, FF_CHUNK, D_MODEL).astype(BF16)
    return dict(w1=w1, wgt=wgt, wg=wg, bg=b_gate[None], wuqt=wuqt, wuk=wuk, wuvt=wuvt, wbr=wbr,
                wo=w_out.astype(BF16), wup=wup, cw=cw, cb=cb, wd=wd)


def _tiles(seq):
    tm_proj = min(1024, seq)
    tm_ffn = min(512, seq)
    tq = min(2048, seq)
    return tm_proj, tm_ffn, tq


def kernel(x, norm1_g, w_in, b_gate, na_rpb, mla_qa_g, mla_kva_g, mla_w_uq, mla_w_ukv, gqa_sink, w_br_na, w_br_mla, w_br_gqa, w_out, norm2_g, w_up, conv_w, conv_b, w_down, final_g):
    batch, seq, _ = x.shape
    depth = w_in.shape[0]
    rows = seq // GRID_W
    assert seq % (NA_ROWS_PER_STEP * GRID_W) == 0 and rows >= 12 and rows % 2 == 0
    assert seq % (GQA_BLOCKS_PER_STEP * BLOCK) == 0 and MLA_ROPE == GQA_HEAD_DIM
    tm, tm_ffn, tq = _tiles(seq)
    assert tm % MLA_KC == 0 and seq % tq == 0 and seq % tm == 0 and seq % tm_ffn == 0
    rope, rope_t = _rope_tables(seq)
    gqa_bias = _gqa_mask_bias(seq // BLOCK)
    xf = x.reshape(batch * seq, D_MODEL)
    for l in range(depth):
        p = _prep_layer(w_in[l], b_gate[l], mla_w_uq[l], mla_w_ukv[l], w_br_na[l], w_br_mla[l],
                        w_br_gqa[l], w_out[l], w_up[l], conv_w[l], conv_b[l], w_down[l])
        g1 = norm1_g[l][None]
        naqt, nak, navt, mqt, mk, mvt, gqt, gk, gvt = _inproj(
            xf, g1, p["w1"], mla_qa_g[l][None], mla_kva_g[l][None], p["wuqt"], p["wuk"], p["wuvt"],
            p["wgt"], rope, rope_t, seq=seq, tm=tm)
        y_na_t = _na(naqt, nak, navt, _na_bias_table(na_rpb[l], rows), batch=batch, seq=seq)
        y_mla = _mla(mqt, mk, mvt, batch=batch, seq=seq, tq=tq).reshape(batch * seq, -1)
        sink = jnp.repeat(gqa_sink[l].astype(F32).reshape(GQA_KV_HEADS, 1, GQA_GROUP), BLOCK, axis=-1)
        y_gqa_t = _gqa(gqt, gk, gvt, gqa_bias, sink, batch=batch, seq=seq)
        xf = _mixout(xf, y_na_t, y_mla, y_gqa_t, g1, p["wg"], p["bg"], p["wbr"], p["wo"], tm=tm)
        xf = _ffn(xf, norm2_g[l][None], p["wup"], p["cw"], p["cb"], p["wd"], final_g[None],
                  seq=seq, tm=tm_ffn, final=(l == depth - 1))
    return xf.reshape(batch, seq, D_MODEL)
```

```python
import functools

import jax
import jax.numpy as jnp
import numpy as np
from jax import lax
from jax.experimental import pallas as pl
from jax.experimental.pallas import tpu as pltpu

F32 = jnp.float32
BF16 = jnp.bfloat16

D_MODEL = 1024
GRID_W = 64
NA_HEADS = 8
NA_HEAD_DIM = 64
NA_WIN_R = 8
NA_WIN_C = 16
NA_W = NA_HEADS * NA_HEAD_DIM
MLA_HEADS = 4
MLA_Q_RANK = 384
MLA_KV_RANK = 256
MLA_NOPE = 128
MLA_ROPE = 64
MLA_V = 128
MLA_QK = MLA_NOPE + MLA_ROPE
MLA_QK_PAD = 256
MLA_KC = 512
MLA_VT_ROWS = MLA_V + 16
MLA_AHEAD = 1
MLA_LOG2_SCALE = float(MLA_QK ** -0.5 * np.log2(np.e))
GQA_HEADS = 8
GQA_KV_HEADS = 2
GQA_GROUP = GQA_HEADS // GQA_KV_HEADS
GQA_HEAD_DIM = 64
GQA_WINDOW = 128
GQA_Q_W = GQA_HEADS * GQA_HEAD_DIM
GQA_KV_W = GQA_KV_HEADS * GQA_HEAD_DIM
BLOCK = 128
N_BRANCH = 3
D_FF = 2816
ROPE_THETA = 10000.0
EPS = 1e-6
NEG_INF = -1e30

LANES = 128
ONES_ROWS = 16
IN1_W = 1408
IN2_W = 1664
FF_CHUNK = 256
N_FF_CHUNKS = D_FF // FF_CHUNK
HALO = 16
VMEM_LIMIT = 56 * 1024 * 1024

_O_NAK = 0
_O_CQ = 512
_O_CKV = 896
_O_GK = 1152
_O_KR = 1280
_T_NAQ = 0
_T_NAV = 512
_T_GQ = 1024
_T_GV = 1536


def _rms(x, g):
    return x * lax.rsqrt(jnp.mean(x * x, axis=-1, keepdims=True) + EPS) * g


def _rope128(blk, c, s1, s2):
    return blk * c + pltpu.roll(blk, 32, 1) * s1 + pltpu.roll(blk, 96, 1) * s2


def _dot(a, b):
    return jnp.dot(a, b, preferred_element_type=F32)


def _dot_nt(a, b):
    return lax.dot_general(a, b, (((1,), (1,)), ((), ())), preferred_element_type=F32)


def _dot_tn(a, b):
    return lax.dot_general(a, b, (((0,), (0,)), ((), ())), preferred_element_type=F32)


def _ones_rows(n):
    return (lax.broadcasted_iota(jnp.int32, (ONES_ROWS, n), 0) == 0).astype(BF16)


def _params(n_axes):
    return pltpu.CompilerParams(dimension_semantics=("arbitrary",) * n_axes,
                                vmem_limit_bytes=VMEM_LIMIT)


def _const_spec(shape):
    n = len(shape)
    return pl.BlockSpec(shape, lambda *_: (0,) * n, pipeline_mode=pl.Buffered(1))


def _inproj_kernel(x_ref, g1_ref, w1_ref, gqa_ref, gkva_ref, wuqt_ref, wuk_ref, wuvt_ref, wgt_ref,
                   rope_ref, ropet_ref, naqt_ref, nak_ref, navt_ref, mqt_ref, mk_ref, mvt_ref,
                   gqt_ref, gk_ref, gvt_ref):
    tm = x_ref.shape[0]
    h = _rms(x_ref[...], g1_ref[...]).astype(BF16)
    z = _dot(h, w1_ref[...])
    gt = _dot_nt(wgt_ref[...], h)

    naqt_ref[...] = gt[_T_NAQ:_T_NAV, :].astype(BF16)
    for c in range(tm // LANES):
        navt_ref[c] = gt[_T_NAV:_T_GQ, c * LANES:(c + 1) * LANES].astype(BF16)
    nak_ref[...] = z[:, _O_NAK:_O_CQ].astype(BF16)

    mc, ms1, ms2 = rope_ref[:, 0:128], rope_ref[:, 128:256], rope_ref[:, 256:384]
    gc, gs1, gs2 = rope_ref[:, 384:512], rope_ref[:, 512:640], rope_ref[:, 640:768]

    cqn = _rms(z[:, _O_CQ:_O_CKV], gqa_ref[...]).astype(BF16)
    qt = _dot_nt(wuqt_ref[...], cqn) * MLA_LOG2_SCALE
    half = MLA_ROPE // 2
    ct, st = ropet_ref[0:half, :], ropet_ref[half:MLA_ROPE, :]
    for hd in range(MLA_HEADS):
        o = hd * MLA_QK_PAD
        r = o + MLA_NOPE
        mqt_ref[o:r, :] = qt[o:r, :].astype(BF16)
        x1, x2 = qt[r:r + half, :], qt[r + half:r + MLA_ROPE, :]
        mqt_ref[r:r + half, :] = (x1 * ct - x2 * st).astype(BF16)
        mqt_ref[r + half:r + MLA_ROPE, :] = (x1 * st + x2 * ct).astype(BF16)
        mqt_ref[r + MLA_ROPE:o + MLA_QK_PAD, :] = qt[r + MLA_ROPE:o + MLA_QK_PAD, :].astype(BF16)

    ckvn = _rms(z[:, _O_CKV:_O_GK], gkva_ref[...]).astype(BF16)
    kn = _dot(ckvn, wuk_ref[...])
    kpe = _rope128(z[:, _O_KR:_O_KR + LANES], mc, ms1, ms2).astype(BF16)
    for hd in range(MLA_HEADS):
        o = hd * MLA_QK_PAD
        mk_ref[:, o:o + LANES] = kn[:, hd * LANES:(hd + 1) * LANES].astype(BF16)
        mk_ref[:, o + LANES:o + 2 * LANES] = kpe
    vt = _dot_nt(wuvt_ref[...], ckvn)
    ones_rows = _ones_rows(MLA_KC)
    for hd in range(MLA_HEADS):
        for c in range(tm // MLA_KC):
            mvt_ref[hd, c, 0:MLA_V, :] = vt[hd * MLA_V:(hd + 1) * MLA_V,
                                            c * MLA_KC:(c + 1) * MLA_KC].astype(BF16)
            mvt_ref[hd, c, MLA_V:MLA_VT_ROWS, :] = ones_rows

    for hd in range(GQA_HEADS):
        r = _T_GQ + hd * GQA_HEAD_DIM
        x1, x2 = gt[r:r + half, :], gt[r + half:r + GQA_HEAD_DIM, :]
        o = hd * GQA_HEAD_DIM
        gqt_ref[o:o + half, :] = (x1 * ct - x2 * st).astype(BF16)
        gqt_ref[o + half:o + GQA_HEAD_DIM, :] = (x1 * st + x2 * ct).astype(BF16)
    gvt_ref[...] = gt[_T_GV:, :].astype(BF16)
    gk_ref[...] = _rope128(z[:, _O_GK:_O_KR], gc, gs1, gs2).astype(BF16)


def _inproj(x2d, g1, w1, g_qa, g_kva, wuqt, wuk, wuvt, wgt, rope, ropet, *, seq, tm):
    T = x2d.shape[0]
    tiles_per_seq = seq // tm
    cpt = tm // MLA_KC
    row = lambda w: pl.BlockSpec((tm, w), lambda i: (i, 0))
    col = lambda w: pl.BlockSpec((w, tm), lambda i: (0, i))
    qk_w = MLA_HEADS * MLA_QK_PAD
    out_specs = [col(NA_W), row(NA_W),
                 pl.BlockSpec((tm // LANES, NA_W, LANES), lambda i: (i, 0, 0)),
                 col(qk_w), row(qk_w),
                 pl.BlockSpec((MLA_HEADS, cpt, MLA_VT_ROWS, MLA_KC), lambda i: (0, i, 0, 0)),
                 col(GQA_Q_W), row(GQA_KV_W), col(GQA_KV_W)]
    out_shape = [jax.ShapeDtypeStruct((NA_W, T), BF16),
                 jax.ShapeDtypeStruct((T, NA_W), BF16),
                 jax.ShapeDtypeStruct((T // LANES, NA_W, LANES), BF16),
                 jax.ShapeDtypeStruct((qk_w, T), BF16),
                 jax.ShapeDtypeStruct((T, qk_w), BF16),
                 jax.ShapeDtypeStruct((MLA_HEADS, T // MLA_KC, MLA_VT_ROWS, MLA_KC), BF16),
                 jax.ShapeDtypeStruct((GQA_Q_W, T), BF16),
                 jax.ShapeDtypeStruct((T, GQA_KV_W), BF16),
                 jax.ShapeDtypeStruct((GQA_KV_W, T), BF16)]
    return pl.pallas_call(
        _inproj_kernel,
        grid=(T // tm,),
        in_specs=[row(D_MODEL), _const_spec((1, D_MODEL)), _const_spec((D_MODEL, IN1_W)),
                  _const_spec((1, MLA_Q_RANK)), _const_spec((1, MLA_KV_RANK)),
                  _const_spec((qk_w, MLA_Q_RANK)),
                  _const_spec((MLA_KV_RANK, MLA_HEADS * MLA_NOPE)),
                  _const_spec((MLA_HEADS * MLA_V, MLA_KV_RANK)),
                  _const_spec((IN2_W, D_MODEL)),
                  pl.BlockSpec((tm, 6 * LANES), lambda i: (i % tiles_per_seq, 0)),
                  pl.BlockSpec((MLA_ROPE, tm), lambda i: (0, i % tiles_per_seq))],
        out_specs=out_specs,
        out_shape=out_shape,
        compiler_params=_params(1),
        name="inproj",
    )(x2d, g1, w1, g_qa, g_kva, wuqt, wuk, wuvt, wgt, rope, ropet)


NA_ROWS_PER_STEP = 16
NA_KEY_ROWS = 10
NA_CASES = 5
NA_AHEAD = 3


def _na_kernel(qt_ref, k_ref, vt_ref, tb_ref, o_ref, *, rows):
    step = pl.program_id(1)
    n_keys = NA_KEY_ROWS * GRID_W
    ones_rows = _ones_rows(n_keys)
    zeros = jnp.zeros((NA_HEAD_DIM, LANES), BF16)
    pairs = []
    for rp in range(NA_ROWS_PER_STEP // 2):
        r = step * NA_ROWS_PER_STEP + 2 * rp
        case = jnp.where(r == 0, 0, jnp.where(r == 2, 1, jnp.where(
            r == rows - 4, 3, jnp.where(r == rows - 2, 4, 2))))
        u = jnp.clip(r - NA_WIN_R // 2, 0, rows - NA_KEY_ROWS)
        pairs.append((case, pl.multiple_of(u * GRID_W, LANES), u // 2))

    def scores(rp, hp):
        case, tok0, _ = pairs[rp]
        lo, mid, hi = hp * LANES, hp * LANES + NA_HEAD_DIM, (hp + 1) * LANES
        qcols = slice(rp * LANES, (rp + 1) * LANES)
        qa, qb = qt_ref[lo:mid, qcols], qt_ref[mid:hi, qcols]
        rhs = jnp.concatenate([jnp.concatenate([qa, zeros], axis=1),
                               jnp.concatenate([zeros, qb], axis=1)], axis=0)
        bias = jnp.concatenate([tb_ref[case, 2 * hp], tb_ref[case, 2 * hp + 1]], axis=1)
        return _dot(k_ref[pl.ds(tok0, n_keys), lo:hi], rhs) + bias

    def finish(rp, hp, t):
        _, _, c0 = pairs[rp]
        lo, mid, hi = hp * LANES, hp * LANES + NA_HEAD_DIM, (hp + 1) * LANES
        qcols = slice(rp * LANES, (rp + 1) * LANES)
        m = jnp.max(t, axis=0, keepdims=True)
        p = jnp.exp(t - m).astype(BF16)
        vt_win = jnp.concatenate([vt_ref[c0 + j, lo:hi, :] for j in range(n_keys // LANES)], axis=1)
        lhs = jnp.concatenate([vt_win, ones_rows], axis=0)
        acc = _dot(lhs, p)
        l = acc[LANES:LANES + 1, :]
        o_ref[lo:mid, qcols] = (acc[0:NA_HEAD_DIM, 0:LANES] / l[:, 0:LANES]).astype(BF16)
        o_ref[mid:hi, qcols] = (acc[NA_HEAD_DIM:LANES, LANES:] / l[:, LANES:]).astype(BF16)

    units = [(rp, hp) for rp in range(NA_ROWS_PER_STEP // 2) for hp in range(NA_HEADS // 2)]
    pending = [scores(*u) for u in units[:NA_AHEAD]]
    for idx, unit in enumerate(units):
        if idx + NA_AHEAD < len(units):
            pending.append(scores(*units[idx + NA_AHEAD]))
        finish(*unit, pending.pop(0))


def _na(naqt, nak, navt, tb, *, batch, seq):
    rows = seq // GRID_W
    tq = NA_ROWS_PER_STEP * GRID_W
    n_steps = seq // tq
    T = batch * seq
    k3 = nak.reshape(batch, seq, NA_W)
    vt4 = navt.reshape(batch, seq // LANES, NA_W, LANES)
    return pl.pallas_call(
        functools.partial(_na_kernel, rows=rows),
        grid=(batch, n_steps),
        in_specs=[pl.BlockSpec((NA_W, tq), lambda b, i: (0, b * n_steps + i)),
                  pl.BlockSpec((None, seq, NA_W), lambda b, i: (b, 0, 0), pipeline_mode=pl.Buffered(1)),
                  pl.BlockSpec((None, seq // LANES, NA_W, LANES), lambda b, i: (b, 0, 0, 0),
                               pipeline_mode=pl.Buffered(1)),
                  _const_spec(tb.shape)],
        out_specs=pl.BlockSpec((NA_W, tq), lambda b, i: (0, b * n_steps + i)),
        out_shape=jax.ShapeDtypeStruct((NA_W, T), BF16),
        compiler_params=_params(2),
        name="na",
    )(naqt, k3, vt4, tb)


def _na_bias_table(rpb, rows):
    kc = np.arange(GRID_W)[:, None]
    c = np.arange(GRID_W)[None, :]
    c_start = np.clip(c - NA_WIN_C // 2, 0, GRID_W - NA_WIN_C)
    valid_c = (kc >= c_start) & (kc < c_start + NA_WIN_C)
    side = GRID_W - NA_WIN_C
    padded = jnp.pad(rpb.astype(F32), ((0, 0), (0, 0), (side, side)))
    toe = jnp.stack([padded[:, :, GRID_W - 1 - cc:2 * GRID_W - 1 - cc] for cc in range(GRID_W)], axis=-1)
    toe = jnp.where(valid_c[None, None], toe, NEG_INF)
    neg = jnp.full((rpb.shape[0], GRID_W, GRID_W), NEG_INF, F32)
    tabs = []
    for r in (0, 2, 4, rows - 4, rows - 2):
        u = min(max(r - NA_WIN_R // 2, 0), rows - NA_KEY_ROWS)
        key_rows = []
        for i in range(NA_KEY_ROWS):
            blocks = []
            for rr in range(2):
                r_start = min(max(r + rr - NA_WIN_R // 2, 0), rows - NA_WIN_R)
                inside = r_start <= u + i < r_start + NA_WIN_R
                blocks.append(toe[:, u + i - (r + rr) + (NA_WIN_R - 1)] if inside else neg)
            key_rows.append(jnp.concatenate(blocks, axis=-1))
        tabs.append(jnp.concatenate(key_rows, axis=1))
    return jnp.stack(tabs)


def _mla_kernel(qt_ref, k_ref, vt_ref, o_ref, *, n_chunks):
    qt = qt_ref[...]
    tq = qt.shape[1]

    def scores(c):
        k0 = pl.multiple_of(c * MLA_KC, MLA_KC)
        return _dot(k_ref[pl.ds(k0, MLA_KC), :], qt)

    def update(c, m, acc, t):
        m_new = jnp.maximum(m, jnp.max(t, axis=0, keepdims=True))
        alpha = jnp.exp2(m - m_new)
        p = jnp.exp2(t - m_new).astype(BF16)
        return m_new, alpha * acc + _dot(vt_ref[c], p)

    m = jnp.full((1, tq), -jnp.inf, F32)
    acc = jnp.zeros((MLA_VT_ROWS, tq), F32)
    pending = [scores(c) for c in range(MLA_AHEAD)]
    for c in range(n_chunks):
        if c + MLA_AHEAD < n_chunks:
            pending.append(scores(c + MLA_AHEAD))
        m, acc = update(c, m, acc, pending.pop(0))
    o = acc[0:MLA_V, :] / acc[MLA_V:MLA_V + 1, :]
    o_ref[...] = o.T.astype(BF16)


def _mla(mqt, mk, mvt, *, batch, seq, tq):
    n_chunks = seq // MLA_KC
    nq = seq // tq
    k3 = mk.reshape(batch, seq, MLA_HEADS * MLA_QK_PAD)
    return pl.pallas_call(
        functools.partial(_mla_kernel, n_chunks=n_chunks),
        grid=(batch, MLA_HEADS, nq),
        in_specs=[pl.BlockSpec((MLA_QK_PAD, tq), lambda b, h, i: (h, b * nq + i)),
                  pl.BlockSpec((None, seq, MLA_QK_PAD), lambda b, h, i: (b, 0, h)),
                  pl.BlockSpec((None, n_chunks, MLA_VT_ROWS, MLA_KC), lambda b, h, i: (h, b, 0, 0))],
        out_specs=pl.BlockSpec((None, tq, MLA_V), lambda b, h, i: (b, i, h)),
        out_shape=jax.ShapeDtypeStruct((batch, seq, MLA_HEADS * MLA_V), BF16),
        compiler_params=_params(3),
        name="mla",
    )(mqt, k3, mvt)


GQA_BLOCKS_PER_STEP = 8
GQA_AHEAD = 1


def _gqa_kernel(qt_ref, kp_ref, kc_ref, kn_ref, vp_ref, vc_ref, vn_ref, b_ref, sk_ref, o_ref, *,
                n_steps):
    i = pl.program_id(1)
    nbs = GQA_BLOCKS_PER_STEP
    k_all = jnp.concatenate([kp_ref[...], kc_ref[...], kn_ref[...]], axis=0)
    vt_all = jnp.concatenate([vp_ref[...], vc_ref[...], vn_ref[...]], axis=1)
    ones_rows = _ones_rows(3 * BLOCK)
    zeros = jnp.zeros((GQA_HEAD_DIM, GQA_GROUP * BLOCK), BF16)

    def scores(j, kvh):
        if j == 0:
            bias = b_ref[jnp.where(i == 0, 0, 1)]
        elif j == nbs - 1:
            bias = b_ref[jnp.where(i == n_steps - 1, 2, 1)]
        else:
            bias = b_ref[1]
        bias = jnp.concatenate([bias] * GQA_GROUP, axis=1)
        heads = range(kvh * GQA_GROUP, (kvh + 1) * GQA_GROUP)
        qs = jnp.concatenate([qt_ref[hd * GQA_HEAD_DIM:(hd + 1) * GQA_HEAD_DIM,
                                     j * BLOCK:(j + 1) * BLOCK] for hd in heads], axis=1)
        rhs = jnp.concatenate([qs, zeros] if kvh == 0 else [zeros, qs], axis=0)
        return _dot(k_all[j * BLOCK:(j + 3) * BLOCK, :], rhs) + bias

    def finish(j, kvh, t):
        sink = sk_ref[kvh]
        m = jnp.maximum(jnp.max(t, axis=0, keepdims=True), sink)
        p = jnp.exp(t - m).astype(BF16)
        vt2 = vt_all[kvh * GQA_HEAD_DIM:(kvh + 1) * GQA_HEAD_DIM, j * BLOCK:(j + 3) * BLOCK]
        acc = _dot(jnp.concatenate([vt2, ones_rows], axis=0), p)
        l = acc[GQA_HEAD_DIM:GQA_HEAD_DIM + 1, :] + jnp.exp(sink - m)
        o = acc[0:GQA_HEAD_DIM, :] / l
        for g in range(GQA_GROUP):
            hd = kvh * GQA_GROUP + g
            o_ref[hd * GQA_HEAD_DIM:(hd + 1) * GQA_HEAD_DIM, j * BLOCK:(j + 1) * BLOCK] = (
                o[:, g * BLOCK:(g + 1) * BLOCK].astype(BF16))

    units = [(j, kvh) for j in range(nbs) for kvh in range(GQA_KV_HEADS)]
    pending = [scores(*u) for u in units[:GQA_AHEAD]]
    for idx, unit in enumerate(units):
        if idx + GQA_AHEAD < len(units):
            pending.append(scores(*units[idx + GQA_AHEAD]))
        finish(*unit, pending.pop(0))


def _gqa(gqt, gk, gvt, bias3, sink, *, batch, seq):
    nb = seq // BLOCK
    nbs = GQA_BLOCKS_PER_STEP
    n_steps = nb // nbs
    T = batch * seq
    k3 = gk.reshape(batch, seq, GQA_KV_W)
    prev = lambda i: jnp.maximum(i * nbs - 1, 0)
    nxt = lambda i: jnp.minimum((i + 1) * nbs, nb - 1)
    return pl.pallas_call(
        functools.partial(_gqa_kernel, n_steps=n_steps),
        grid=(batch, n_steps),
        in_specs=[pl.BlockSpec((GQA_Q_W, nbs * BLOCK), lambda b, i: (0, b * n_steps + i)),
                  pl.BlockSpec((None, BLOCK, GQA_KV_W), lambda b, i: (b, prev(i), 0)),
                  pl.BlockSpec((None, nbs * BLOCK, GQA_KV_W), lambda b, i: (b, i, 0)),
                  pl.BlockSpec((None, BLOCK, GQA_KV_W), lambda b, i: (b, nxt(i), 0)),
                  pl.BlockSpec((GQA_KV_W, BLOCK), lambda b, i: (0, b * nb + prev(i))),
                  pl.BlockSpec((GQA_KV_W, nbs * BLOCK), lambda b, i: (0, b * n_steps + i)),
                  pl.BlockSpec((GQA_KV_W, BLOCK), lambda b, i: (0, b * nb + nxt(i))),
                  _const_spec((3, 3 * BLOCK, BLOCK)),
                  _const_spec((GQA_KV_HEADS, 1, GQA_GROUP * BLOCK))],
        out_specs=pl.BlockSpec((GQA_Q_W, nbs * BLOCK), lambda b, i: (0, b * n_steps + i)),
        out_shape=jax.ShapeDtypeStruct((GQA_Q_W, T), BF16),
        compiler_params=_params(2),
        name="gqa",
    )(gqt, k3, k3, k3, gvt, gvt, gvt, bias3, sink)


def _gqa_mask_bias(nb):
    i = np.arange(BLOCK)[None, :]
    j = np.arange(3 * BLOCK)[:, None]
    band = np.abs(BLOCK + i - j) <= GQA_WINDOW
    first = band & (j >= BLOCK)
    last = band & (j < 2 * BLOCK)
    if nb == 1:
        first = last = first & last
    m = np.stack([first, band, last])
    return jnp.asarray(np.where(m, 0.0, NEG_INF), F32)


def _mixout_kernel(x_ref, ynat_ref, ymla_ref, ygqat_ref, g1_ref, wg_ref, bg_ref, wbr_ref, wo_ref, o_ref):
    x = x_ref[...]
    h = _rms(x, g1_ref[...]).astype(BF16)
    branches = (_dot_tn(ynat_ref[...], wbr_ref[0]), _dot(ymla_ref[...], wbr_ref[1]),
                _dot_tn(ygqat_ref[...], wbr_ref[2]))
    merged = None
    for i, br in enumerate(branches):
        cols = slice(i * D_MODEL, (i + 1) * D_MODEL)
        gate = jax.nn.sigmoid(_dot(h, wg_ref[:, cols]) + bg_ref[:, cols])
        merged = gate * br if merged is None else merged + gate * br
    o_ref[...] = x + _dot(merged.astype(BF16), wo_ref[...])


def _mixout(x2d, y_na_t, y_mla, y_gqa_t, g1, wg, bg, wbr, wo, *, tm):
    T = x2d.shape[0]
    row = lambda w: pl.BlockSpec((tm, w), lambda i: (i, 0))
    col = lambda w: pl.BlockSpec((w, tm), lambda i: (0, i))
    return pl.pallas_call(
        _mixout_kernel,
        grid=(T // tm,),
        in_specs=[row(D_MODEL), col(NA_W), row(MLA_HEADS * MLA_V), col(GQA_Q_W),
                  _const_spec((1, D_MODEL)), _const_spec((D_MODEL, N_BRANCH * D_MODEL)),
                  _const_spec((1, N_BRANCH * D_MODEL)), _const_spec((N_BRANCH, NA_W, D_MODEL)),
                  _const_spec((D_MODEL, D_MODEL))],
        out_specs=row(D_MODEL),
        out_shape=jax.ShapeDtypeStruct((T, D_MODEL), F32),
        compiler_params=_params(1),
        name="mixout",
    )(x2d, y_na_t, y_mla, y_gqa_t, g1, wg, bg, wbr, wo)


def _ffn_kernel(x_ref, xp_ref, xn_ref, g2_ref, wup_ref, cw_ref, cb_ref, wd_ref, fg_ref, o_ref,
                hx_ref, u_ref, acc_ref, *, tm, tiles_per_seq, final):
    j = pl.program_id(0) % tiles_per_seq
    g2 = g2_ref[...]
    x = x_ref[...]
    hp = jnp.where(j == 0, 0.0, _rms(xp_ref[...], g2))
    hn = jnp.where(j == tiles_per_seq - 1, 0.0, _rms(xn_ref[...], g2))
    hx_ref[0:HALO, :] = hp.astype(BF16)
    hx_ref[HALO:HALO + tm, :] = _rms(x, g2).astype(BF16)
    hx_ref[HALO + tm:, :] = hn.astype(BF16)

    def up(c):
        u_ref[c % 2] = _dot(hx_ref[...], wup_ref[c])

    def down(c):
        ub = u_ref.at[c % 2]
        cw = cw_ref[c]
        y = (cw[0:1] * ub[pl.ds(HALO - 1, tm), :] + cw[1:2] * ub[pl.ds(HALO, tm), :]
             + cw[2:3] * ub[pl.ds(HALO + 1, tm), :] + cb_ref[c])
        act = jax.nn.gelu(y[:, FF_CHUNK:]) * y[:, :FF_CHUNK]
        d = _dot(act.astype(BF16), wd_ref[c])
        if c == 0:
            acc_ref[...] = d
        else:
            acc_ref[...] += d

    up(0)
    for c in range(N_FF_CHUNKS):
        if c + 1 < N_FF_CHUNKS:
            up(c + 1)
        down(c)
    out = x + acc_ref[...]
    if final:
        out = _rms(out, fg_ref[...])
    o_ref[...] = out


def _ffn(x2d, g2, wup, cw, cb, wd, fg, *, seq, tm, final):
    T = x2d.shape[0]
    tiles_per_seq = seq // tm
    hb = tm // HALO
    n_hb = T // HALO
    return pl.pallas_call(
        functools.partial(_ffn_kernel, tm=tm, tiles_per_seq=tiles_per_seq, final=final),
        grid=(T // tm,),
        in_specs=[pl.BlockSpec((tm, D_MODEL), lambda i: (i, 0)),
                  pl.BlockSpec((HALO, D_MODEL), lambda i: (jnp.maximum(i * hb - 1, 0), 0)),
                  pl.BlockSpec((HALO, D_MODEL), lambda i: (jnp.minimum((i + 1) * hb, n_hb - 1), 0)),
                  _const_spec((1, D_MODEL)),
                  _const_spec((N_FF_CHUNKS, D_MODEL, 2 * FF_CHUNK)),
                  _const_spec((N_FF_CHUNKS, 3, 2 * FF_CHUNK)),
                  _const_spec((N_FF_CHUNKS, 1, 2 * FF_CHUNK)),
                  _const_spec((N_FF_CHUNKS, FF_CHUNK, D_MODEL)),
                  _const_spec((1, D_MODEL))],
        out_specs=pl.BlockSpec((tm, D_MODEL), lambda i: (i, 0)),
        out_shape=jax.ShapeDtypeStruct((T, D_MODEL), F32),
        scratch_shapes=[pltpu.VMEM((tm + 2 * HALO, D_MODEL), BF16),
                        pltpu.VMEM((2, tm + 2 * HALO, 2 * FF_CHUNK), F32),
                        pltpu.VMEM((tm, D_MODEL), F32)],
        compiler_params=_params(1),
        name="ffn",
    )(x2d, x2d, x2d, g2, wup, cw, cb, wd, fg)


def _rope_tables(seq):
    def cs(dim):
        inv = 1.0 / (ROPE_THETA ** (jnp.arange(0, dim, 2, dtype=F32) / dim))
        ang = jnp.arange(seq, dtype=F32)[:, None] * inv[None, :]
        return jnp.cos(ang), jnp.sin(ang)

    def head(c, s):
        z = jnp.zeros_like(s)
        return (jnp.concatenate([c, c], -1), jnp.concatenate([z, s], -1), jnp.concatenate([-s, z], -1))

    m_cos, m_sin = cs(MLA_ROPE)
    mc, ms1, ms2 = head(m_cos, m_sin)
    pad = jnp.zeros_like(mc)
    gc, gs1, gs2 = head(*cs(GQA_HEAD_DIM))
    two = lambda a: jnp.concatenate([a, a], -1)
    table = jnp.concatenate([mc, pad, ms1, pad, ms2, pad, two(gc), two(gs1), two(gs2)], axis=-1)
    table_t = jnp.concatenate([m_cos, m_sin], axis=-1).T
    return table, table_t


def _prep_layer(w_in, b_gate, mla_w_uq, mla_w_ukv, w_br_na, w_br_mla, w_br_gqa, w_out,
                w_up, conv_w, conv_b, w_down):
    o_nak = NA_W
    o_nav = 2 * NA_W
    o_cq = 3 * NA_W
    o_ckv = o_cq + MLA_Q_RANK
    o_kr = o_ckv + MLA_KV_RANK
    o_gq = o_kr + MLA_ROPE
    o_gkv = o_gq + GQA_Q_W
    o_gv = o_gkv + GQA_KV_W
    o_gate = o_gv + GQA_KV_W
    na_scale = NA_HEAD_DIM ** -0.5
    gqa_scale = GQA_HEAD_DIM ** -0.5
    w1 = jnp.concatenate([
        w_in[:, o_nak:o_nav], w_in[:, o_cq:o_ckv], w_in[:, o_ckv:o_kr], w_in[:, o_gkv:o_gv],
        w_in[:, o_kr:o_gq], jnp.zeros((D_MODEL, LANES - MLA_ROPE), w_in.dtype)], axis=1).astype(BF16)
    wgt = jnp.concatenate([w_in[:, 0:o_nak] * na_scale, w_in[:, o_nav:o_cq],
                           w_in[:, o_gq:o_gkv] * gqa_scale, w_in[:, o_gv:o_gate]], axis=1).T.astype(BF16)
    wg = w_in[:, o_gate:].astype(BF16)
    wuqt = jnp.pad(mla_w_uq.reshape(MLA_Q_RANK, MLA_HEADS, MLA_QK),
                   ((0, 0), (0, 0), (0, MLA_QK_PAD - MLA_QK))).reshape(MLA_Q_RANK, -1).T.astype(BF16)
    ukv = mla_w_ukv.reshape(MLA_KV_RANK, MLA_HEADS, MLA_NOPE + MLA_V)
    wuk = ukv[:, :, :MLA_NOPE].reshape(MLA_KV_RANK, -1).astype(BF16)
    wuvt = ukv[:, :, MLA_NOPE:].reshape(MLA_KV_RANK, -1).T.astype(BF16)
    wbr = jnp.stack([w_br_na, w_br_mla, w_br_gqa]).astype(BF16)
    chunks = lambda a: a.reshape(a.shape[0], 2, N_FF_CHUNKS, FF_CHUNK)
    wup = chunks(w_up).transpose(2, 0, 1, 3).reshape(N_FF_CHUNKS, D_MODEL, 2 * FF_CHUNK).astype(BF16)
    cw = chunks(conv_w).transpose(2, 0, 1, 3).reshape(N_FF_CHUNKS, 3, 2 * FF_CHUNK)
    cb = chunks(conv_b[None]).transpose(2, 0, 1, 3).reshape(N_FF_CHUNKS, 1, 2 * FF_CHUNK)
    wd = w_down.reshape(N_FF_CHUNKS, FF_CHUNK, D_MODEL).astype(BF16)
    return dict(w1=w1, wgt=wgt, wg=wg, bg=b_gate[None], wuqt=wuqt, wuk=wuk, wuvt=wuvt, wbr=wbr,
                wo=w_out.astype(BF16), wup=wup, cw=cw, cb=cb, wd=wd)


def _tiles(seq):
    tm_proj = min(1024, seq)
    tm_ffn = min(512, seq)
    tq = min(2048, seq)
    return tm_proj, tm_ffn, tq


def kernel(x, norm1_g, w_in, b_gate, na_rpb, mla_qa_g, mla_kva_g, mla_w_uq, mla_w_ukv, gqa_sink, w_br_na, w_br_mla, w_br_gqa, w_out, norm2_g, w_up, conv_w, conv_b, w_down, final_g):
    batch, seq, _ = x.shape
    depth = w_in.shape[0]
    rows = seq // GRID_W
    assert seq % (NA_ROWS_PER_STEP * GRID_W) == 0 and rows >= 12 and rows % 2 == 0
    assert seq % (GQA_BLOCKS_PER_STEP * BLOCK) == 0 and MLA_ROPE == GQA_HEAD_DIM
    tm, tm_ffn, tq = _tiles(seq)
    assert tm % MLA_KC == 0 and seq % tq == 0 and seq % tm == 0 and seq % tm_ffn == 0
    rope, rope_t = _rope_tables(seq)
    gqa_bias = _gqa_mask_bias(seq // BLOCK)
    xf = x.reshape(batch * seq, D_MODEL)
    for l in range(depth):
        p = _prep_layer(w_in[l], b_gate[l], mla_w_uq[l], mla_w_ukv[l], w_br_na[l], w_br_mla[l],
                        w_br_gqa[l], w_out[l], w_up[l], conv_w[l], conv_b[l], w_down[l])
        g1 = norm1_g[l][None]
        naqt, nak, navt, mqt, mk, mvt, gqt, gk, gvt = _inproj(
            xf, g1, p["w1"], mla_qa_g[l][None], mla_kva_g[l][None], p["wuqt"], p["wuk"], p["wuvt"],
            p["wgt"], rope, rope_t, seq=seq, tm=tm)
        y_na_t = _na(naqt, nak, navt, _na_bias_table(na_rpb[l], rows), batch=batch, seq=seq)
        y_mla = _mla(mqt, mk, mvt, batch=batch, seq=seq, tq=tq).reshape(batch * seq, -1)
        sink = jnp.repeat(gqa_sink[l].astype(F32).reshape(GQA_KV_HEADS, 1, GQA_GROUP), BLOCK, axis=-1)
        y_gqa_t = _gqa(gqt, gk, gvt, gqa_bias, sink, batch=batch, seq=seq)
        xf = _mixout(xf, y_na_t, y_mla, y_gqa_t, g1, p["wg"], p["bg"], p["wbr"], p["wo"], tm=tm)
        xf = _ffn(xf, norm2_g[l][None], p["wup"], p["cw"], p["cb"], p["wd"], final_g[None],
                  seq=seq, tm=tm_ffn, final=(l == depth - 1))
    return xf.reshape(batch, seq, D_MODEL)
```

```python
import functools

import jax
import jax.numpy as jnp
import numpy as np
from jax import lax
from jax.experimental import pallas as pl
from jax.experimental.pallas import tpu as pltpu

F32 = jnp.float32
BF16 = jnp.bfloat16

D_MODEL = 1024
GRID_W = 64
NA_HEADS = 8
NA_HEAD_DIM = 64
NA_WIN_R = 8
NA_WIN_C = 16
NA_W = NA_HEADS * NA_HEAD_DIM
MLA_HEADS = 4
MLA_Q_RANK = 384
MLA_KV_RANK = 256
MLA_NOPE = 128
MLA_ROPE = 64
MLA_V = 128
MLA_QK = MLA_NOPE + MLA_ROPE
MLA_QK_PAD = 256
MLA_KC = 512
MLA_VT_ROWS = MLA_V + 16
MLA_AHEAD = 1
MLA_LOG2_SCALE = float(MLA_QK ** -0.5 * np.log2(np.e))
GQA_HEADS = 8
GQA_KV_HEADS = 2
GQA_GROUP = GQA_HEADS // GQA_KV_HEADS
GQA_HEAD_DIM = 64
GQA_WINDOW = 128
GQA_Q_W = GQA_HEADS * GQA_HEAD_DIM
GQA_KV_W = GQA_KV_HEADS * GQA_HEAD_DIM
BLOCK = 128
N_BRANCH = 3
D_FF = 2816
ROPE_THETA = 10000.0
EPS = 1e-6
NEG_INF = -1e30

LANES = 128
ONES_ROWS = 16
IN1_W = 1408
IN2_W = 1664
FF_CHUNK = 256
N_FF_CHUNKS = D_FF // FF_CHUNK
FF_AHEAD = 2
FF_BUFS = FF_AHEAD + 1
HALO = 16
VMEM_LIMIT = 56 * 1024 * 1024

_O_NAK = 0
_O_CQ = 512
_O_CKV = 896
_O_GK = 1152
_O_KR = 1280
_T_NAQ = 0
_T_NAV = 512
_T_GQ = 1024
_T_GV = 1536


def _rms(x, g):
    return x * lax.rsqrt(jnp.mean(x * x, axis=-1, keepdims=True) + EPS) * g


def _rope128(blk, c, s1, s2):
    return blk * c + pltpu.roll(blk, 32, 1) * s1 + pltpu.roll(blk, 96, 1) * s2


def _dot(a, b):
    return jnp.dot(a, b, preferred_element_type=F32)


def _dot_nt(a, b):
    return lax.dot_general(a, b, (((1,), (1,)), ((), ())), preferred_element_type=F32)


def _dot_tn(a, b):
    return lax.dot_general(a, b, (((0,), (0,)), ((), ())), preferred_element_type=F32)


def _ones_rows(n):
    return (lax.broadcasted_iota(jnp.int32, (ONES_ROWS, n), 0) == 0).astype(BF16)


def _params(n_axes):
    return pltpu.CompilerParams(dimension_semantics=("arbitrary",) * n_axes,
                                vmem_limit_bytes=VMEM_LIMIT)


def _const_spec(shape):
    n = len(shape)
    return pl.BlockSpec(shape, lambda *_: (0,) * n, pipeline_mode=pl.Buffered(1))


def _inproj_kernel(x_ref, g1_ref, w1_ref, gqa_ref, gkva_ref, wuqt_ref, wuk_ref, wuvt_ref, wgt_ref,
                   rope_ref, ropet_ref, naqt_ref, nak_ref, navt_ref, mqt_ref, mk_ref, mvt_ref,
                   gqt_ref, gk_ref, gvt_ref):
    tm = x_ref.shape[0]
    h = _rms(x_ref[...], g1_ref[...]).astype(BF16)
    z = _dot(h, w1_ref[...])
    gt = _dot_nt(wgt_ref[...], h)

    naqt_ref[...] = gt[_T_NAQ:_T_NAV, :].astype(BF16)
    for c in range(tm // LANES):
        navt_ref[c] = gt[_T_NAV:_T_GQ, c * LANES:(c + 1) * LANES].astype(BF16)
    nak_ref[...] = z[:, _O_NAK:_O_CQ].astype(BF16)

    mc, ms1, ms2 = rope_ref[:, 0:128], rope_ref[:, 128:256], rope_ref[:, 256:384]
    gc, gs1, gs2 = rope_ref[:, 384:512], rope_ref[:, 512:640], rope_ref[:, 640:768]

    cqn = _rms(z[:, _O_CQ:_O_CKV], gqa_ref[...]).astype(BF16)
    qt = _dot_nt(wuqt_ref[...], cqn) * MLA_LOG2_SCALE
    half = MLA_ROPE // 2
    ct, st = ropet_ref[0:half, :], ropet_ref[half:MLA_ROPE, :]
    for hd in range(MLA_HEADS):
        o = hd * MLA_QK_PAD
        r = o + MLA_NOPE
        mqt_ref[o:r, :] = qt[o:r, :].astype(BF16)
        x1, x2 = qt[r:r + half, :], qt[r + half:r + MLA_ROPE, :]
        mqt_ref[r:r + half, :] = (x1 * ct - x2 * st).astype(BF16)
        mqt_ref[r + half:r + MLA_ROPE, :] = (x1 * st + x2 * ct).astype(BF16)
        mqt_ref[r + MLA_ROPE:o + MLA_QK_PAD, :] = qt[r + MLA_ROPE:o + MLA_QK_PAD, :].astype(BF16)

    ckvn = _rms(z[:, _O_CKV:_O_GK], gkva_ref[...]).astype(BF16)
    kn = _dot(ckvn, wuk_ref[...])
    kpe = _rope128(z[:, _O_KR:_O_KR + LANES], mc, ms1, ms2).astype(BF16)
    for hd in range(MLA_HEADS):
        o = hd * MLA_QK_PAD
        mk_ref[:, o:o + LANES] = kn[:, hd * LANES:(hd + 1) * LANES].astype(BF16)
        mk_ref[:, o + LANES:o + 2 * LANES] = kpe
    vt = _dot_nt(wuvt_ref[...], ckvn)
    ones_rows = _ones_rows(MLA_KC)
    for hd in range(MLA_HEADS):
        for c in range(tm // MLA_KC):
            mvt_ref[hd, c, 0:MLA_V, :] = vt[hd * MLA_V:(hd + 1) * MLA_V,
                                            c * MLA_KC:(c + 1) * MLA_KC].astype(BF16)
            mvt_ref[hd, c, MLA_V:MLA_VT_ROWS, :] = ones_rows

    for hd in range(GQA_HEADS):
        r = _T_GQ + hd * GQA_HEAD_DIM
        x1, x2 = gt[r:r + half, :], gt[r + half:r + GQA_HEAD_DIM, :]
        o = hd * GQA_HEAD_DIM
        gqt_ref[o:o + half, :] = (x1 * ct - x2 * st).astype(BF16)
        gqt_ref[o + half:o + GQA_HEAD_DIM, :] = (x1 * st + x2 * ct).astype(BF16)
    gvt_ref[...] = gt[_T_GV:, :].astype(BF16)
    gk_ref[...] = _rope128(z[:, _O_GK:_O_KR], gc, gs1, gs2).astype(BF16)


def _inproj(x2d, g1, w1, g_qa, g_kva, wuqt, wuk, wuvt, wgt, rope, ropet, *, seq, tm):
    T = x2d.shape[0]
    tiles_per_seq = seq // tm
    cpt = tm // MLA_KC
    row = lambda w: pl.BlockSpec((tm, w), lambda i: (i, 0))
    col = lambda w: pl.BlockSpec((w, tm), lambda i: (0, i))
    qk_w = MLA_HEADS * MLA_QK_PAD
    out_specs = [col(NA_W), row(NA_W),
                 pl.BlockSpec((tm // LANES, NA_W, LANES), lambda i: (i, 0, 0)),
                 col(qk_w), row(qk_w),
                 pl.BlockSpec((MLA_HEADS, cpt, MLA_VT_ROWS, MLA_KC), lambda i: (0, i, 0, 0)),
                 col(GQA_Q_W), row(GQA_KV_W), col(GQA_KV_W)]
    out_shape = [jax.ShapeDtypeStruct((NA_W, T), BF16),
                 jax.ShapeDtypeStruct((T, NA_W), BF16),
                 jax.ShapeDtypeStruct((T // LANES, NA_W, LANES), BF16),
                 jax.ShapeDtypeStruct((qk_w, T), BF16),
                 jax.ShapeDtypeStruct((T, qk_w), BF16),
                 jax.ShapeDtypeStruct((MLA_HEADS, T // MLA_KC, MLA_VT_ROWS, MLA_KC), BF16),
                 jax.ShapeDtypeStruct((GQA_Q_W, T), BF16),
                 jax.ShapeDtypeStruct((T, GQA_KV_W), BF16),
                 jax.ShapeDtypeStruct((GQA_KV_W, T), BF16)]
    return pl.pallas_call(
        _inproj_kernel,
        grid=(T // tm,),
        in_specs=[row(D_MODEL), _const_spec((1, D_MODEL)), _const_spec((D_MODEL, IN1_W)),
                  _const_spec((1, MLA_Q_RANK)), _const_spec((1, MLA_KV_RANK)),
                  _const_spec((qk_w, MLA_Q_RANK)),
                  _const_spec((MLA_KV_RANK, MLA_HEADS * MLA_NOPE)),
                  _const_spec((MLA_HEADS * MLA_V, MLA_KV_RANK)),
                  _const_spec((IN2_W, D_MODEL)),
                  pl.BlockSpec((tm, 6 * LANES), lambda i: (i % tiles_per_seq, 0)),
                  pl.BlockSpec((MLA_ROPE, tm), lambda i: (0, i % tiles_per_seq))],
        out_specs=out_specs,
        out_shape=out_shape,
        compiler_params=_params(1),
        name="inproj",
    )(x2d, g1, w1, g_qa, g_kva, wuqt, wuk, wuvt, wgt, rope, ropet)


NA_ROWS_PER_STEP = 16
NA_KEY_ROWS = 10
NA_CASES = 5
NA_AHEAD = 3


def _na_kernel(qt_ref, k_ref, vt_ref, tb_ref, o_ref, *, rows):
    step = pl.program_id(1)
    n_keys = NA_KEY_ROWS * GRID_W
    ones_rows = _ones_rows(n_keys)
    zeros = jnp.zeros((NA_HEAD_DIM, LANES), BF16)
    pairs = []
    for rp in range(NA_ROWS_PER_STEP // 2):
        r = step * NA_ROWS_PER_STEP + 2 * rp
        case = jnp.where(r == 0, 0, jnp.where(r == 2, 1, jnp.where(
            r == rows - 4, 3, jnp.where(r == rows - 2, 4, 2))))
        u = jnp.clip(r - NA_WIN_R // 2, 0, rows - NA_KEY_ROWS)
        pairs.append((case, pl.multiple_of(u * GRID_W, LANES), u // 2))

    def scores(rp, hp):
        case, tok0, _ = pairs[rp]
        lo, mid, hi = hp * LANES, hp * LANES + NA_HEAD_DIM, (hp + 1) * LANES
        qcols = slice(rp * LANES, (rp + 1) * LANES)
        qa, qb = qt_ref[lo:mid, qcols], qt_ref[mid:hi, qcols]
        rhs = jnp.concatenate([jnp.concatenate([qa, zeros], axis=1),
                               jnp.concatenate([zeros, qb], axis=1)], axis=0)
        bias = jnp.concatenate([tb_ref[case, 2 * hp], tb_ref[case, 2 * hp + 1]], axis=1)
        return _dot(k_ref[pl.ds(tok0, n_keys), lo:hi], rhs) + bias

    def finish(rp, hp, t):
        _, _, c0 = pairs[rp]
        lo, mid, hi = hp * LANES, hp * LANES + NA_HEAD_DIM, (hp + 1) * LANES
        qcols = slice(rp * LANES, (rp + 1) * LANES)
        m = jnp.max(t, axis=0, keepdims=True)
        p = jnp.exp(t - m).astype(BF16)
        vt_win = jnp.concatenate([vt_ref[c0 + j, lo:hi, :] for j in range(n_keys // LANES)], axis=1)
        lhs = jnp.concatenate([vt_win, ones_rows], axis=0)
        acc = _dot(lhs, p)
        l = acc[LANES:LANES + 1, :]
        o_ref[lo:mid, qcols] = (acc[0:NA_HEAD_DIM, 0:LANES] / l[:, 0:LANES]).astype(BF16)
        o_ref[mid:hi, qcols] = (acc[NA_HEAD_DIM:LANES, LANES:] / l[:, LANES:]).astype(BF16)

    units = [(rp, hp) for rp in range(NA_ROWS_PER_STEP // 2) for hp in range(NA_HEADS // 2)]
    pending = [scores(*u) for u in units[:NA_AHEAD]]
    for idx, unit in enumerate(units):
        if idx + NA_AHEAD < len(units):
            pending.append(scores(*units[idx + NA_AHEAD]))
        finish(*unit, pending.pop(0))


def _na(naqt, nak, navt, tb, *, batch, seq):
    rows = seq // GRID_W
    tq = NA_ROWS_PER_STEP * GRID_W
    n_steps = seq // tq
    T = batch * seq
    k3 = nak.reshape(batch, seq, NA_W)
    vt4 = navt.reshape(batch, seq // LANES, NA_W, LANES)
    return pl.pallas_call(
        functools.partial(_na_kernel, rows=rows),
        grid=(batch, n_steps),
        in_specs=[pl.BlockSpec((NA_W, tq), lambda b, i: (0, b * n_steps + i)),
                  pl.BlockSpec((None, seq, NA_W), lambda b, i: (b, 0, 0), pipeline_mode=pl.Buffered(1)),
                  pl.BlockSpec((None, seq // LANES, NA_W, LANES), lambda b, i: (b, 0, 0, 0),
                               pipeline_mode=pl.Buffered(1)),
                  _const_spec(tb.shape)],
        out_specs=pl.BlockSpec((NA_W, tq), lambda b, i: (0, b * n_steps + i)),
        out_shape=jax.ShapeDtypeStruct((NA_W, T), BF16),
        compiler_params=_params(2),
        name="na",
    )(naqt, k3, vt4, tb)


def _na_bias_table(rpb, rows):
    kc = np.arange(GRID_W)[:, None]
    c = np.arange(GRID_W)[None, :]
    c_start = np.clip(c - NA_WIN_C // 2, 0, GRID_W - NA_WIN_C)
    valid_c = (kc >= c_start) & (kc < c_start + NA_WIN_C)
    side = GRID_W - NA_WIN_C
    padded = jnp.pad(rpb.astype(F32), ((0, 0), (0, 0), (side, side)))
    toe = jnp.stack([padded[:, :, GRID_W - 1 - cc:2 * GRID_W - 1 - cc] for cc in range(GRID_W)], axis=-1)
    toe = jnp.where(valid_c[None, None], toe, NEG_INF)
    neg = jnp.full((rpb.shape[0], GRID_W, GRID_W), NEG_INF, F32)
    tabs = []
    for r in (0, 2, 4, rows - 4, rows - 2):
        u = min(max(r - NA_WIN_R // 2, 0), rows - NA_KEY_ROWS)
        key_rows = []
        for i in range(NA_KEY_ROWS):
            blocks = []
            for rr in range(2):
                r_start = min(max(r + rr - NA_WIN_R // 2, 0), rows - NA_WIN_R)
                inside = r_start <= u + i < r_start + NA_WIN_R
                blocks.append(toe[:, u + i - (r + rr) + (NA_WIN_R - 1)] if inside else neg)
            key_rows.append(jnp.concatenate(blocks, axis=-1))
        tabs.append(jnp.concatenate(key_rows, axis=1))
    return jnp.stack(tabs)


def _mla_kernel(qt_ref, k_ref, vt_ref, o_ref, *, n_chunks):
    qt = qt_ref[...]
    tq = qt.shape[1]

    def scores(c):
        k0 = pl.multiple_of(c * MLA_KC, MLA_KC)
        return _dot(k_ref[pl.ds(k0, MLA_KC), :], qt)

    def update(c, m, acc, t):
        m_new = jnp.maximum(m, jnp.max(t, axis=0, keepdims=True))
        alpha = jnp.exp2(m - m_new)
        p = jnp.exp2(t - m_new).astype(BF16)
        return m_new, alpha * acc + _dot(vt_ref[c], p)

    m = jnp.full((1, tq), -jnp.inf, F32)
    acc = jnp.zeros((MLA_VT_ROWS, tq), F32)
    pending = [scores(c) for c in range(MLA_AHEAD)]
    for c in range(n_chunks):
        if c + MLA_AHEAD < n_chunks:
            pending.append(scores(c + MLA_AHEAD))
        m, acc = update(c, m, acc, pending.pop(0))
    o = acc[0:MLA_V, :] / acc[MLA_V:MLA_V + 1, :]
    o_ref[...] = o.T.astype(BF16)


def _mla(mqt, mk, mvt, *, batch, seq, tq):
    n_chunks = seq // MLA_KC
    nq = seq // tq
    k3 = mk.reshape(batch, seq, MLA_HEADS * MLA_QK_PAD)
    return pl.pallas_call(
        functools.partial(_mla_kernel, n_chunks=n_chunks),
        grid=(batch, MLA_HEADS, nq),
        in_specs=[pl.BlockSpec((MLA_QK_PAD, tq), lambda b, h, i: (h, b * nq + i)),
                  pl.BlockSpec((None, seq, MLA_QK_PAD), lambda b, h, i: (b, 0, h)),
                  pl.BlockSpec((None, n_chunks, MLA_VT_ROWS, MLA_KC), lambda b, h, i: (h, b, 0, 0))],
        out_specs=pl.BlockSpec((None, tq, MLA_V), lambda b, h, i: (b, i, h)),
        out_shape=jax.ShapeDtypeStruct((batch, seq, MLA_HEADS * MLA_V), BF16),
        compiler_params=_params(3),
        name="mla",
    )(mqt, k3, mvt)


GQA_BLOCKS_PER_STEP = 8
GQA_AHEAD = 1


def _gqa_kernel(qt_ref, kp_ref, kc_ref, kn_ref, vp_ref, vc_ref, vn_ref, b_ref, sk_ref, o_ref, *,
                n_steps):
    i = pl.program_id(1)
    nbs = GQA_BLOCKS_PER_STEP
    k_all = jnp.concatenate([kp_ref[...], kc_ref[...], kn_ref[...]], axis=0)
    vt_all = jnp.concatenate([vp_ref[...], vc_ref[...], vn_ref[...]], axis=1)
    ones_rows = _ones_rows(3 * BLOCK)
    zeros = jnp.zeros((GQA_HEAD_DIM, GQA_GROUP * BLOCK), BF16)

    def scores(j, kvh):
        if j == 0:
            bias = b_ref[jnp.where(i == 0, 0, 1)]
        elif j == nbs - 1:
            bias = b_ref[jnp.where(i == n_steps - 1, 2, 1)]
        else:
            bias = b_ref[1]
        bias = jnp.concatenate([bias] * GQA_GROUP, axis=1)
        heads = range(kvh * GQA_GROUP, (kvh + 1) * GQA_GROUP)
        qs = jnp.concatenate([qt_ref[hd * GQA_HEAD_DIM:(hd + 1) * GQA_HEAD_DIM,
                                     j * BLOCK:(j + 1) * BLOCK] for hd in heads], axis=1)
        rhs = jnp.concatenate([qs, zeros] if kvh == 0 else [zeros, qs], axis=0)
        return _dot(k_all[j * BLOCK:(j + 3) * BLOCK, :], rhs) + bias

    def finish(j, kvh, t):
        sink = sk_ref[kvh]
        m = jnp.maximum(jnp.max(t, axis=0, keepdims=True), sink)
        p = jnp.exp(t - m).astype(BF16)
        vt2 = vt_all[kvh * GQA_HEAD_DIM:(kvh + 1) * GQA_HEAD_DIM, j * BLOCK:(j + 3) * BLOCK]
        acc = _dot(jnp.concatenate([vt2, ones_rows], axis=0), p)
        l = acc[GQA_HEAD_DIM:GQA_HEAD_DIM + 1, :] + jnp.exp(sink - m)
        o = acc[0:GQA_HEAD_DIM, :] / l
        for g in range(GQA_GROUP):
            hd = kvh * GQA_GROUP + g
            o_ref[hd * GQA_HEAD_DIM:(hd + 1) * GQA_HEAD_DIM, j * BLOCK:(j + 1) * BLOCK] = (
                o[:, g * BLOCK:(g + 1) * BLOCK].astype(BF16))

    units = [(j, kvh) for j in range(nbs) for kvh in range(GQA_KV_HEADS)]
    pending = [scores(*u) for u in units[:GQA_AHEAD]]
    for idx, unit in enumerate(units):
        if idx + GQA_AHEAD < len(units):
            pending.append(scores(*units[idx + GQA_AHEAD]))
        finish(*unit, pending.pop(0))


def _gqa(gqt, gk, gvt, bias3, sink, *, batch, seq):
    nb = seq // BLOCK
    nbs = GQA_BLOCKS_PER_STEP
    n_steps = nb // nbs
    T = batch * seq
    k3 = gk.reshape(batch, seq, GQA_KV_W)
    prev = lambda i: jnp.maximum(i * nbs - 1, 0)
    nxt = lambda i: jnp.minimum((i + 1) * nbs, nb - 1)
    return pl.pallas_call(
        functools.partial(_gqa_kernel, n_steps=n_steps),
        grid=(batch, n_steps),
        in_specs=[pl.BlockSpec((GQA_Q_W, nbs * BLOCK), lambda b, i: (0, b * n_steps + i)),
                  pl.BlockSpec((None, BLOCK, GQA_KV_W), lambda b, i: (b, prev(i), 0)),
                  pl.BlockSpec((None, nbs * BLOCK, GQA_KV_W), lambda b, i: (b, i, 0)),
                  pl.BlockSpec((None, BLOCK, GQA_KV_W), lambda b, i: (b, nxt(i), 0)),
                  pl.BlockSpec((GQA_KV_W, BLOCK), lambda b, i: (0, b * nb + prev(i))),
                  pl.BlockSpec((GQA_KV_W, nbs * BLOCK), lambda b, i: (0, b * n_steps + i)),
                  pl.BlockSpec((GQA_KV_W, BLOCK), lambda b, i: (0, b * nb + nxt(i))),
                  _const_spec((3, 3 * BLOCK, BLOCK)),
                  _const_spec((GQA_KV_HEADS, 1, GQA_GROUP * BLOCK))],
        out_specs=pl.BlockSpec((GQA_Q_W, nbs * BLOCK), lambda b, i: (0, b * n_steps + i)),
        out_shape=jax.ShapeDtypeStruct((GQA_Q_W, T), BF16),
        compiler_params=_params(2),
        name="gqa",
    )(gqt, k3, k3, k3, gvt, gvt, gvt, bias3, sink)


def _gqa_mask_bias(nb):
    i = np.arange(BLOCK)[None, :]
    j = np.arange(3 * BLOCK)[:, None]
    band = np.abs(BLOCK + i - j) <= GQA_WINDOW
    first = band & (j >= BLOCK)
    last = band & (j < 2 * BLOCK)
    if nb == 1:
        first = last = first & last
    m = np.stack([first, band, last])
    return jnp.asarray(np.where(m, 0.0, NEG_INF), F32)


def _mixout_kernel(x_ref, ynat_ref, ymla_ref, ygqat_ref, g1_ref, wg_ref, bg_ref, wbr_ref, wo_ref, o_ref):
    x = x_ref[...]
    h = _rms(x, g1_ref[...]).astype(BF16)
    branches = (_dot_tn(ynat_ref[...], wbr_ref[0]), _dot(ymla_ref[...], wbr_ref[1]),
                _dot_tn(ygqat_ref[...], wbr_ref[2]))
    merged = None
    for i, br in enumerate(branches):
        cols = slice(i * D_MODEL, (i + 1) * D_MODEL)
        gate = jax.nn.sigmoid(_dot(h, wg_ref[:, cols]) + bg_ref[:, cols])
        merged = gate * br if merged is None else merged + gate * br
    o_ref[...] = x + _dot(merged.astype(BF16), wo_ref[...])


def _mixout(x2d, y_na_t, y_mla, y_gqa_t, g1, wg, bg, wbr, wo, *, tm):
    T = x2d.shape[0]
    row = lambda w: pl.BlockSpec((tm, w), lambda i: (i, 0))
    col = lambda w: pl.BlockSpec((w, tm), lambda i: (0, i))
    return pl.pallas_call(
        _mixout_kernel,
        grid=(T // tm,),
        in_specs=[row(D_MODEL), col(NA_W), row(MLA_HEADS * MLA_V), col(GQA_Q_W),
                  _const_spec((1, D_MODEL)), _const_spec((D_MODEL, N_BRANCH * D_MODEL)),
                  _const_spec((1, N_BRANCH * D_MODEL)), _const_spec((N_BRANCH, NA_W, D_MODEL)),
                  _const_spec((D_MODEL, D_MODEL))],
        out_specs=row(D_MODEL),
        out_shape=jax.ShapeDtypeStruct((T, D_MODEL), F32),
        compiler_params=_params(1),
        name="mixout",
    )(x2d, y_na_t, y_mla, y_gqa_t, g1, wg, bg, wbr, wo)


def _ffn_kernel(x_ref, xp_ref, xn_ref, g2_ref, wup_ref, cw_ref, cb_ref, wd_ref, fg_ref, o_ref,
                hx_ref, u_ref, acc_ref, *, tm, tiles_per_seq, final):
    j = pl.program_id(0) % tiles_per_seq
    g2 = g2_ref[...]
    x = x_ref[...]
    hp = jnp.where(j == 0, 0.0, _rms(xp_ref[...], g2))
    hn = jnp.where(j == tiles_per_seq - 1, 0.0, _rms(xn_ref[...], g2))
    hx_ref[0:HALO, :] = hp.astype(BF16)
    hx_ref[HALO:HALO + tm, :] = _rms(x, g2).astype(BF16)
    hx_ref[HALO + tm:, :] = hn.astype(BF16)

    def up(c):
        u_ref[c % FF_BUFS] = _dot(hx_ref[...], wup_ref[c])

    def down(c):
        ub = u_ref.at[c % FF_BUFS]
        cw = cw_ref[c]
        y = (cw[0:1] * ub[pl.ds(HALO - 1, tm), :] + cw[1:2] * ub[pl.ds(HALO, tm), :]
             + cw[2:3] * ub[pl.ds(HALO + 1, tm), :] + cb_ref[c])
        act = jax.nn.gelu(y[:, FF_CHUNK:]) * y[:, :FF_CHUNK]
        d = _dot(act.astype(BF16), wd_ref[c])
        if c == 0:
            acc_ref[...] = d
        else:
            acc_ref[...] += d

    for c in range(FF_AHEAD):
        up(c)
    for c in range(N_FF_CHUNKS):
        if c + FF_AHEAD < N_FF_CHUNKS:
            up(c + FF_AHEAD)
        down(c)
    out = x + acc_ref[...]
    if final:
        out = _rms(out, fg_ref[...])
    o_ref[...] = out


def _ffn(x2d, g2, wup, cw, cb, wd, fg, *, seq, tm, final):
    T = x2d.shape[0]
    tiles_per_seq = seq // tm
    hb = tm // HALO
    n_hb = T // HALO
    return pl.pallas_call(
        functools.partial(_ffn_kernel, tm=tm, tiles_per_seq=tiles_per_seq, final=final),
        grid=(T // tm,),
        in_specs=[pl.BlockSpec((tm, D_MODEL), lambda i: (i, 0)),
                  pl.BlockSpec((HALO, D_MODEL), lambda i: (jnp.maximum(i * hb - 1, 0), 0)),
                  pl.BlockSpec((HALO, D_MODEL), lambda i: (jnp.minimum((i + 1) * hb, n_hb - 1), 0)),
                  _const_spec((1, D_MODEL)),
                  _const_spec((N_FF_CHUNKS, D_MODEL, 2 * FF_CHUNK)),
                  _const_spec((N_FF_CHUNKS, 3, 2 * FF_CHUNK)),
                  _const_spec((N_FF_CHUNKS, 1, 2 * FF_CHUNK)),
                  _const_spec((N_FF_CHUNKS, FF_CHUNK, D_MODEL)),
                  _const_spec((1, D_MODEL))],
        out_specs=pl.BlockSpec((tm, D_MODEL), lambda i: (i, 0)),
        out_shape=jax.ShapeDtypeStruct((T, D_MODEL), F32),
        scratch_shapes=[pltpu.VMEM((tm + 2 * HALO, D_MODEL), BF16),
                        pltpu.VMEM((FF_BUFS, tm + 2 * HALO, 2 * FF_CHUNK), F32),
                        pltpu.VMEM((tm, D_MODEL), F32)],
        compiler_params=_params(1),
        name="ffn",
    )(x2d, x2d, x2d, g2, wup, cw, cb, wd, fg)


def _rope_tables(seq):
    def cs(dim):
        inv = 1.0 / (ROPE_THETA ** (jnp.arange(0, dim, 2, dtype=F32) / dim))
        ang = jnp.arange(seq, dtype=F32)[:, None] * inv[None, :]
        return jnp.cos(ang), jnp.sin(ang)

    def head(c, s):
        z = jnp.zeros_like(s)
        return (jnp.concatenate([c, c], -1), jnp.concatenate([z, s], -1), jnp.concatenate([-s, z], -1))

    m_cos, m_sin = cs(MLA_ROPE)
    mc, ms1, ms2 = head(m_cos, m_sin)
    pad = jnp.zeros_like(mc)
    gc, gs1, gs2 = head(*cs(GQA_HEAD_DIM))
    two = lambda a: jnp.concatenate([a, a], -1)
    table = jnp.concatenate([mc, pad, ms1, pad, ms2, pad, two(gc), two(gs1), two(gs2)], axis=-1)
    table_t = jnp.concatenate([m_cos, m_sin], axis=-1).T
    return table, table_t


def _prep_layer(w_in, b_gate, mla_w_uq, mla_w_ukv, w_br_na, w_br_mla, w_br_gqa, w_out,
                w_up, conv_w, conv_b, w_down):
    o_nak = NA_W
    o_nav = 2 * NA_W
    o_cq = 3 * NA_W
    o_ckv = o_cq + MLA_Q_RANK
    o_kr = o_ckv + MLA_KV_RANK
    o_gq = o_kr + MLA_ROPE
    o_gkv = o_gq + GQA_Q_W
    o_gv = o_gkv + GQA_KV_W
    o_gate = o_gv + GQA_KV_W
    na_scale = NA_HEAD_DIM ** -0.5
    gqa_scale = GQA_HEAD_DIM ** -0.5
    w1 = jnp.concatenate([
        w_in[:, o_nak:o_nav], w_in[:, o_cq:o_ckv], w_in[:, o_ckv:o_kr], w_in[:, o_gkv:o_gv],
        w_in[:, o_kr:o_gq], jnp.zeros((D_MODEL, LANES - MLA_ROPE), w_in.dtype)], axis=1).astype(BF16)
    wgt = jnp.concatenate([w_in[:, 0:o_nak] * na_scale, w_in[:, o_nav:o_cq],
                           w_in[:, o_gq:o_gkv] * gqa_scale, w_in[:, o_gv:o_gate]], axis=1).T.astype(BF16)
    wg = w_in[:, o_gate:].astype(BF16)
    wuqt = jnp.pad(mla_w_uq.reshape(MLA_Q_RANK, MLA_HEADS, MLA_QK),
                   ((0, 0), (0, 0), (0, MLA_QK_PAD - MLA_QK))).reshape(MLA_Q_RANK, -1).T.astype(BF16)
    ukv = mla_w_ukv.reshape(MLA_KV_RANK, MLA_HEADS, MLA_NOPE + MLA_V)
    wuk = ukv[:, :, :MLA_NOPE].reshape(MLA_KV_RANK, -1).astype(BF16)
    wuvt = ukv[:, :, MLA_NOPE:].reshape(MLA_KV_RANK, -1).T.astype(BF16)
    wbr = jnp.stack([w_br_na, w_br_mla, w_br_gqa]).astype(BF16)
    chunks = lambda a: a.reshape(a.shape[0], 2, N_FF_CHUNKS, FF_CHUNK)
    wup = chunks(w_up).transpose(2, 0, 1, 3).reshape(N_FF_CHUNKS, D_MODEL, 2 * FF_CHUNK).astype(BF16)
    cw = chunks(conv_w).transpose(2, 0, 1, 3).reshape(N_FF_CHUNKS, 3, 2 * FF_CHUNK)
    cb = chunks(conv_b[None]).transpose(2, 0, 1, 3).reshape(N_FF_CHUNKS, 1, 2 * FF_CHUNK)
    wd = w_down.reshape(N_FF_CHUNKS, FF_CHUNK, D_MODEL).astype(BF16)
    return dict(w1=w1, wgt=wgt, wg=wg, bg=b_gate[None], wuqt=wuqt, wuk=wuk, wuvt=wuvt, wbr=wbr,
                wo=w_out.astype(BF16), wup=wup, cw=cw, cb=cb, wd=wd)


def _tiles(seq):
    tm_proj = min(1024, seq)
    tm_ffn = min(512, seq)
    tq = min(2048, seq)
    return tm_proj, tm_ffn, tq


def kernel(x, norm1_g, w_in, b_gate, na_rpb, mla_qa_g, mla_kva_g, mla_w_uq, mla_w_ukv, gqa_sink, w_br_na, w_br_mla, w_br_gqa, w_out, norm2_g, w_up, conv_w, conv_b, w_down, final_g):
    batch, seq, _ = x.shape
    depth = w_in.shape[0]
    rows = seq // GRID_W
    assert seq % (NA_ROWS_PER_STEP * GRID_W) == 0 and rows >= 12 and rows % 2 == 0
    assert seq % (GQA_BLOCKS_PER_STEP * BLOCK) == 0 and MLA_ROPE == GQA_HEAD_DIM
    tm, tm_ffn, tq = _tiles(seq)
    assert tm % MLA_KC == 0 and seq % tq == 0 and seq % tm == 0 and seq % tm_ffn == 0
    rope, rope_t = _rope_tables(seq)
    gqa_bias = _gqa_mask_bias(seq // BLOCK)
    xf = x.reshape(batch * seq, D_MODEL)
    for l in range(depth):
        p = _prep_layer(w_in[l], b_gate[l], mla_w_uq[l], mla_w_ukv[l], w_br_na[l], w_br_mla[l],
                        w_br_gqa[l], w_out[l], w_up[l], conv_w[l], conv_b[l], w_down[l])
        g1 = norm1_g[l][None]
        naqt, nak, navt, mqt, mk, mvt, gqt, gk, gvt = _inproj(
            xf, g1, p["w1"], mla_qa_g[l][None], mla_kva_g[l][None], p["wuqt"], p["wuk"], p["wuvt"],
            p["wgt"], rope, rope_t, seq=seq, tm=tm)
        y_na_t = _na(naqt, nak, navt, _na_bias_table(na_rpb[l], rows), batch=batch, seq=seq)
        y_mla = _mla(mqt, mk, mvt, batch=batch, seq=seq, tq=tq).reshape(batch * seq, -1)
        sink = jnp.repeat(gqa_sink[l].astype(F32).reshape(GQA_KV_HEADS, 1, GQA_GROUP), BLOCK, axis=-1)
        y_gqa_t = _gqa(gqt, gk, gvt, gqa_bias, sink, batch=batch, seq=seq)
        xf = _mixout(xf, y_na_t, y_mla, y_gqa_t, g1, p["wg"], p["bg"], p["wbr"], p["wo"], tm=tm)
        xf = _ffn(xf, norm2_g[l][None], p["wup"], p["cw"], p["cb"], p["wd"], final_g[None],
                  seq=seq, tm=tm_ffn, final=(l == depth - 1))
    return xf.reshape(batch, seq, D_MODEL)
```
